```python
import math
import jax
import jax.numpy as jnp
from jax import lax
import numpy as np

D_MODEL = 1024
BATCH = 8
SEQ = 4096
DEPTH = 2

CHUNK = 64
D_MIX = D_MODEL
N_MIXERS = 4
GROUP_W = D_MIX // N_MIXERS

S5_CH = 16
S5_G = GROUP_W // S5_CH
S5_P = 64

RW_HD = 64
RW_H = GROUP_W // RW_HD
RW_W_LORA = 64
RW_A_LORA = 64
RW_G_LORA = 128
RW_IN = 3 * GROUP_W + RW_W_LORA + RW_A_LORA + RW_G_LORA
RW_GN_EPS = 64e-5

SB_HD = 64
SB_H = GROUP_W // SB_HD
SB_QBLOCK = 128

ML_HD = 64
ML_H = GROUP_W // ML_HD
ML_CONV = 4

D_FF = 2816
FFN_CONV = 3

LN_EPS = 1e-5
DN_ALPHA = (2 * DEPTH) ** 0.25
DN_BETA = (8 * DEPTH) ** -0.25

IN_SIZES = (GROUP_W, RW_IN, GROUP_W, GROUP_W, GROUP_W, GROUP_W, GROUP_W, GROUP_W, GROUP_W, ML_H, ML_H)
N_IN = sum(IN_SIZES)
RW_SIZES = (GROUP_W, GROUP_W, GROUP_W, RW_W_LORA, RW_A_LORA, RW_G_LORA)

kernel_name = 'hybrid_s5_rwkv7_stickbreak_mlstm_convffn'


def split_cols(z, sizes):
    cuts = [int(c) for c in np.cumsum(sizes)[:-1]]
    return jnp.split(z, cuts, axis=-1)


def layer_norm(x, g, b):
    xf = x.astype(jnp.float32)
    mu = jnp.mean(xf, -1, keepdims=True)
    var = jnp.mean(jnp.square(xf - mu), -1, keepdims=True)
    return ((xf - mu) * lax.rsqrt(var + LN_EPS) * g + b).astype(x.dtype)


def head_norm(x, g, b, eps):
    xf = x.astype(jnp.float32)
    mu = jnp.mean(xf, -1, keepdims=True)
    var = jnp.mean(jnp.square(xf - mu), -1, keepdims=True)
    y = (xf - mu) * lax.rsqrt(var + eps) * g.reshape(x.shape[-2:])
    if b is not None:
        y = y + b.reshape(x.shape[-2:])
    return y


def token_shift(z):
    return jnp.pad(z, ((0, 0), (1, 0), (0, 0)))[:, :-1]


def causal_dwconv(x, w, b):
    k_w, ch = w.shape
    xp = jnp.pad(x, ((0, 0), (k_w - 1, 0), (0, 0)))
    y = lax.conv_general_dilated(xp, w[:, None, :].astype(x.dtype), window_strides=(1,), padding='VALID',
                                 dimension_numbers=('NWC', 'WIO', 'NWC'), feature_group_count=ch)
    return y + b.astype(x.dtype)


def s5_mixer(u, lam_re, lam_im, log_dt, b_re, b_im, c_re, c_im, d, w_glu, b_glu):
    f32 = jnp.float32
    bsz, seq, _ = u.shape
    uf = u.astype(f32).reshape(bsz, seq, S5_G, S5_CH)
    lr, li = lam_re.astype(f32), lam_im.astype(f32)
    dt = jnp.exp(log_dt.astype(f32))[:, None]
    mag = jnp.exp(lr * dt)
    ab_re, ab_im = mag * jnp.cos(li * dt), mag * jnp.sin(li * dt)
    den = lr * lr + li * li
    z_re = ((ab_re - 1.0) * lr + ab_im * li) / den
    z_im = (ab_im * lr - (ab_re - 1.0) * li) / den
    br, bi = b_re.astype(f32), b_im.astype(f32)
    bb_re = z_re[..., None] * br - z_im[..., None] * bi
    bb_im = z_re[..., None] * bi + z_im[..., None] * br
    bu_re = jnp.einsum('bsgc,gpc->sbgp', uf, bb_re)
    bu_im = jnp.einsum('bsgc,gpc->sbgp', uf, bb_im)
    a_re = jnp.broadcast_to(ab_re, (seq, 1, S5_G, S5_P))
    a_im = jnp.broadcast_to(ab_im, (seq, 1, S5_G, S5_P))

    def combine(e1, e2):
        ar1, ai1, xr1, xi1 = e1
        ar2, ai2, xr2, xi2 = e2
        return (ar1 * ar2 - ai1 * ai2, ar1 * ai2 + ai1 * ar2,
                ar2 * xr1 - ai2 * xi1 + xr2, ar2 * xi1 + ai2 * xr1 + xi2)

    _, _, s_re, s_im = lax.associative_scan(combine, (a_re, a_im, bu_re, bu_im), axis=0)
    y = (jnp.einsum('sbgp,gcp->bsgc', s_re, c_re.astype(f32))
         - jnp.einsum('sbgp,gcp->bsgc', s_im, c_im.astype(f32)))
    y = (y + d.astype(f32).reshape(S5_G, S5_CH) * uf).reshape(bsz, seq, GROUP_W)
    y = jax.nn.gelu(y)
    return y * jax.nn.sigmoid(y @ w_glu.astype(f32) + b_glu.astype(f32))


def rwkv7_mixer(z, mu, w0, w2, a0, a2, g2, k_k, k_a, r_k, ln_g, ln_b):
    f32 = jnp.float32
    bsz, seq, _ = z.shape
    z = z.astype(f32)
    z = z + (token_shift(z) - z) * mu.astype(f32)
    r, k, v, xw, xa, xg = split_cols(z, RW_SIZES)
    w = -jax.nn.softplus(-(w0 + jnp.tanh(xw) @ w2)) - 0.5
    decay = jnp.exp(-jnp.exp(w))
    a = jax.nn.sigmoid(a0 + xa @ a2)
    g = jax.nn.sigmoid(xg) @ g2

    def heads(t):
        return t.reshape(bsz, seq, RW_H, RW_HD)

    r, k, v, decay, a = heads(r), heads(k), heads(v), heads(decay), heads(a)
    kk = k * k_k.reshape(RW_H, RW_HD)
    kk = kk / jnp.maximum(jnp.sqrt(jnp.sum(kk * kk, -1, keepdims=True)), 1e-12)
    k = k * (1.0 + (a - 1.0) * k_a.reshape(RW_H, RW_HD))

    def step(state, inp):
        r_t, w_t, k_t, v_t, kk_t, a_t = inp
        sa = jnp.einsum('bhvk,bhk->bhv', state, -kk_t)
        state = (state * w_t[:, :, None, :] + sa[..., None] * (kk_t * a_t)[:, :, None, :]
                 + v_t[..., None] * k_t[:, :, None, :])
        return state, jnp.einsum('bhvk,bhk->bhv', state, r_t)

    xs = tuple(jnp.moveaxis(t, 1, 0) for t in (r, decay, k, v, kk, a))
    state0 = jnp.zeros((bsz, RW_H, RW_HD, RW_HD), f32)
    _, y = lax.scan(step, state0, xs)
    y = jnp.moveaxis(y, 0, 1)
    y = head_norm(y, ln_g, ln_b, RW_GN_EPS)
    y = y + jnp.sum(r * k * r_k, -1, keepdims=True) * v
    return y.reshape(bsz, seq, GROUP_W) * g


def stick_breaking_attention(q, k, v):
    f32 = jnp.float32
    bsz, seq, _ = q.shape

    def heads(t):
        return t.astype(f32).reshape(bsz, seq, SB_H, SB_HD).transpose(0, 2, 1, 3)

    qh, kh, vh = heads(q) * (SB_HD ** -0.5), heads(k), heads(v)
    key_pos = jnp.arange(seq)

    def block(i):
        qb = lax.dynamic_slice_in_dim(qh, i * SB_QBLOCK, SB_QBLOCK, axis=2)
        logits = jnp.einsum('bhqd,bhkd->bhqk', qb, kh)
        q_pos = i * SB_QBLOCK + jnp.arange(SB_QBLOCK)
        mask = key_pos[None, :] < q_pos[:, None]
        log_stay = jnp.where(mask, jax.nn.log_sigmoid(-logits), 0.0)
        log_after = lax.cumsum(log_stay, axis=3, reverse=True) - log_stay
        weights = jnp.where(mask, jnp.exp(jax.nn.log_sigmoid(logits) + log_after), 0.0)
        return jnp.einsum('bhqk,bhkd->bhqd', weights, vh)

    out = lax.map(block, jnp.arange(seq // SB_QBLOCK))
    return out.transpose(1, 0, 3, 2, 4).reshape(bsz, seq, GROUP_W)


def mlstm_chunkwise(q, k, v, ig, fg):
    bsz, seq, nh, hd = q.shape
    nc = seq // CHUNK

    def chunks(t):
        t = t.reshape((bsz, nc, CHUNK, nh) + t.shape[3:])
        return jnp.moveaxis(t, 3, 1)

    q, k, v = chunks(q), chunks(k), chunks(v)
    ig = chunks(ig)
    b = jnp.cumsum(chunks(jax.nn.log_sigmoid(fg)), axis=-1)
    b_last = b[..., -1]
    g = b_last[..., None] - b + ig
    m_loc = jnp.max(g, -1)
    wk = jnp.exp(g - m_loc[..., None])
    c_loc = jnp.einsum('bhcl,bhclv,bhclk->bhcvk', wk, v, k)
    n_loc = jnp.einsum('bhcl,bhclk->bhck', wk, k)

    def step(carry, inp):
        c_st, n_st, m_st = carry
        cl, nl, ml, bl = inp
        m_new = jnp.maximum(bl + m_st, ml)
        s_old = jnp.exp(bl + m_st - m_new)
        s_new = jnp.exp(ml - m_new)
        c_new = s_old[..., None, None] * c_st + s_new[..., None, None] * cl
        n_new = s_old[..., None] * n_st + s_new[..., None] * nl
        return (c_new, n_new, m_new), (c_st, n_st, m_st)

    init = (jnp.zeros((bsz, nh, hd, hd), jnp.float32), jnp.zeros((bsz, nh, hd), jnp.float32),
            jnp.full((bsz, nh), -jnp.inf, jnp.float32))
    xs = tuple(jnp.moveaxis(t, 2, 0) for t in (c_loc, n_loc, m_loc, b_last))
    _, (c_prev, n_prev, m_prev) = lax.scan(step, init, xs)
    c_prev = jnp.moveaxis(c_prev, 0, 2)
    n_prev = jnp.moveaxis(n_prev, 0, 2)
    m_prev = jnp.moveaxis(m_prev, 0, 2)

    causal = jnp.tril(jnp.ones((CHUNK, CHUNK), bool))
    log_intra = jnp.where(causal, b[..., :, None] - b[..., None, :] + ig[..., None, :], -jnp.inf)
    log_inter = b + m_prev[..., None]
    m_t = jnp.maximum(log_inter, jnp.max(log_intra, -1))
    w_intra = jnp.exp(log_intra - m_t[..., None])
    w_inter = jnp.exp(log_inter - m_t)
    s = jnp.einsum('bhcld,bhcsd->bhcls', q, k) * w_intra
    num = (w_inter[..., None] * jnp.einsum('bhcvk,bhclk->bhclv', c_prev, q)
           + jnp.einsum('bhcls,bhcsv->bhclv', s, v))
    den = w_inter * jnp.einsum('bhck,bhclk->bhcl', n_prev, q) + jnp.sum(s, -1)
    h = num / jnp.maximum(jnp.abs(den), jnp.exp(-m_t))[..., None]
    return jnp.moveaxis(h, 1, 3).reshape(bsz, seq, nh, hd)


def mlstm_mixer(q, k, v, o, i_pre, f_pre, conv_w, conv_b, b_i, b_f, ln_g):
    f32 = jnp.float32
    bsz, seq, _ = q.shape
    qk = jax.nn.silu(causal_dwconv(jnp.concatenate([q, k], -1).astype(f32), conv_w, conv_b))
    q, k = qk[..., :GROUP_W], qk[..., GROUP_W:]

    def heads(t):
        return t.reshape(bsz, seq, ML_H, ML_HD)

    h = mlstm_chunkwise(heads(q), heads(k) * (ML_HD ** -0.5), heads(v.astype(f32)),
                        i_pre.astype(f32) + b_i.astype(f32), f_pre.astype(f32) + b_f.astype(f32))
    h = head_norm(h, ln_g, None, LN_EPS).reshape(bsz, seq, GROUP_W)
    return jax.nn.sigmoid(o.astype(f32)) * h


def conv_ffn(h, w_up, conv_w, conv_b, w_down):
    up = h @ w_up
    u, gate = jnp.split(up, 2, axis=-1)
    u = jax.nn.gelu(causal_dwconv(u, conv_w, conv_b))
    return (u * gate) @ w_down


def setup_inputs(seed: int = 0) -> dict:
    key = jax.random.key(seed)
    keys = iter(jax.random.split(key, 48))
    f32 = jnp.float32
    L = DEPTH

    def nrm(shape, scale):
        return scale * jax.random.normal(next(keys), shape, f32)

    n_idx = jnp.arange(S5_P, dtype=f32)
    ratio = jnp.arange(GROUP_W, dtype=f32) / (GROUP_W - 1)
    return {
        'x': nrm((BATCH, SEQ, D_MODEL), 1.0),
        'ln_in_g': 1.0 + nrm((D_MODEL,), 0.02),
        'ln_in_b': nrm((D_MODEL,), 0.02),
        'w_in': nrm((L, D_MODEL, N_IN), D_MODEL ** -0.5),
        's5_lambda_re': -0.5 + nrm((L, S5_G, S5_P), 0.01),
        's5_lambda_im': math.pi * n_idx + nrm((L, S5_G, S5_P), 0.01),
        's5_log_dt': jax.random.uniform(next(keys), (L, S5_G), f32, math.log(1e-3), math.log(1e-1)),
        's5_b_re': nrm((L, S5_G, S5_P, S5_CH), (2 * S5_CH) ** -0.5),
        's5_b_im': nrm((L, S5_G, S5_P, S5_CH), (2 * S5_CH) ** -0.5),
        's5_c_re': nrm((L, S5_G, S5_CH, S5_P), S5_P ** -0.5),
        's5_c_im': nrm((L, S5_G, S5_CH, S5_P), S5_P ** -0.5),
        's5_d': nrm((L, GROUP_W), 1.0),
        's5_w_glu': nrm((L, GROUP_W, GROUP_W), GROUP_W ** -0.5),
        's5_b_glu': nrm((L, GROUP_W), 0.02),
        'rw_mu': jax.random.uniform(next(keys), (L, RW_IN), f32),
        'rw_w0': (-6.5 + 5.0 * ratio ** 0.85) + nrm((L, GROUP_W), 0.1),
        'rw_w2': nrm((L, RW_W_LORA, GROUP_W), 0.5 * RW_W_LORA ** -0.5),
        'rw_a0': nrm((L, GROUP_W), 0.1),
        'rw_a2': nrm((L, RW_A_LORA, GROUP_W), RW_A_LORA ** -0.5),
        'rw_g2': nrm((L, RW_G_LORA, GROUP_W), RW_G_LORA ** -0.5),
        'rw_k_k': 0.85 + nrm((L, GROUP_W), 0.02),
        'rw_k_a': 1.0 + nrm((L, GROUP_W), 0.02),
        'rw_r_k': nrm((L, RW_H, RW_HD), 0.1),
        'rw_ln_g': 1.0 + nrm((L, GROUP_W), 0.02),
        'rw_ln_b': nrm((L, GROUP_W), 0.02),
        'ml_conv_w': nrm((L, ML_CONV, 2 * GROUP_W), ML_CONV ** -0.5),
        'ml_conv_b': nrm((L, 2 * GROUP_W), 0.02),
        'ml_b_i': nrm((L, ML_H), 0.1),
        'ml_b_f': jnp.linspace(3.0, 6.0, ML_H, dtype=f32) + nrm((L, ML_H), 0.1),
        'ml_ln_g': 1.0 + nrm((L, GROUP_W), 0.02),
        'w_out': nrm((L, D_MIX, D_MODEL), DN_BETA * D_MIX ** -0.5),
        'ln1_g': 1.0 + nrm((L, D_MODEL), 0.02),
        'ln1_b': nrm((L, D_MODEL), 0.02),
        'ffn_w_up': nrm((L, D_MODEL, 2 * D_FF), D_MODEL ** -0.5),
        'ffn_conv_w': nrm((L, FFN_CONV, D_FF), FFN_CONV ** -0.5),
        'ffn_conv_b': nrm((L, D_FF), 0.02),
        'ffn_w_down': nrm((L, D_FF, D_MODEL), DN_BETA * D_FF ** -0.5),
        'ln2_g': 1.0 + nrm((L, D_MODEL), 0.02),
        'ln2_b': nrm((L, D_MODEL), 0.02),
    }


def reference(x, ln_in_g, ln_in_b, w_in,
              s5_lambda_re, s5_lambda_im, s5_log_dt, s5_b_re, s5_b_im, s5_c_re, s5_c_im, s5_d,
              s5_w_glu, s5_b_glu,
              rw_mu, rw_w0, rw_w2, rw_a0, rw_a2, rw_g2, rw_k_k, rw_k_a, rw_r_k, rw_ln_g, rw_ln_b,
              ml_conv_w, ml_conv_b, ml_b_i, ml_b_f, ml_ln_g,
              w_out, ln1_g, ln1_b,
              ffn_w_up, ffn_conv_w, ffn_conv_b, ffn_w_down, ln2_g, ln2_b):
    h = layer_norm(x, ln_in_g, ln_in_b)
    for l in range(DEPTH):
        z = h @ w_in[l]
        (z_s5, z_rw, sb_q, sb_k, sb_v, ml_q, ml_k, ml_v, ml_o, ml_i, ml_f) = split_cols(z, IN_SIZES)
        y_s5 = s5_mixer(z_s5, s5_lambda_re[l], s5_lambda_im[l], s5_log_dt[l], s5_b_re[l], s5_b_im[l],
                        s5_c_re[l], s5_c_im[l], s5_d[l], s5_w_glu[l], s5_b_glu[l])
        y_rw = rwkv7_mixer(z_rw, rw_mu[l], rw_w0[l], rw_w2[l], rw_a0[l], rw_a2[l], rw_g2[l],
                           rw_k_k[l], rw_k_a[l], rw_r_k[l], rw_ln_g[l], rw_ln_b[l])
        y_sb = stick_breaking_attention(sb_q, sb_k, sb_v)
        y_ml = mlstm_mixer(ml_q, ml_k, ml_v, ml_o, ml_i, ml_f, ml_conv_w[l], ml_conv_b[l],
                           ml_b_i[l], ml_b_f[l], ml_ln_g[l])
        y = jnp.concatenate([y_s5, y_rw, y_sb, y_ml], axis=-1).astype(h.dtype)
        h = layer_norm(DN_ALPHA * h + y @ w_out[l], ln1_g[l], ln1_b[l])
        h = layer_norm(DN_ALPHA * h + conv_ffn(h, ffn_w_up[l], ffn_conv_w[l], ffn_conv_b[l], ffn_w_down[l]),
                       ln2_g[l], ln2_b[l])
    return h
```

```python
import functools
import math

import jax
import jax.numpy as jnp
from jax import lax
from jax.experimental import pallas as pl
from jax.experimental.pallas import tpu as pltpu

F32 = jnp.float32
BF16 = jnp.bfloat16

GROUP_W = 256
HEAD_DIM = 64
N_HEADS = GROUP_W // HEAD_DIM
CHUNK = 64
S5_G, S5_P, S5_CH = 16, 64, 16
S5_STATE = S5_G * S5_P
RW_LORA = 128
LN_EPS = 1e-5
RW_GN_EPS = 64e-5
SB_SKIP_LOG = 110.0
VMEM_LIMIT = 56 * 1024 * 1024


def _bdot(a, b):
    return jnp.dot(a.astype(BF16), b.astype(BF16), preferred_element_type=F32)


def _bdot_nt(a, b):
    return lax.dot_general(a.astype(BF16), b.astype(BF16), (((1,), (1,)), ((), ())),
                           preferred_element_type=F32)


def _bdot_tn(a, b):
    return lax.dot_general(a.astype(BF16), b.astype(BF16), (((0,), (0,)), ((), ())),
                           preferred_element_type=F32)


def _split3(x):
    hi = x.astype(BF16)
    r1 = x - hi.astype(F32)
    mid = r1.astype(BF16)
    lo = (r1 - mid.astype(F32)).astype(BF16)
    return hi, mid, lo


def _xdot(a, sel):
    hi, mid, lo = _split3(a)
    s = sel.astype(BF16)
    out = jnp.dot(lo, s, preferred_element_type=F32)
    out = out + jnp.dot(mid, s, preferred_element_type=F32)
    return out + jnp.dot(hi, s, preferred_element_type=F32)


def _xdot_left(sel, a):
    hi, mid, lo = _split3(a)
    s = sel.astype(BF16)
    out = jnp.dot(s, lo, preferred_element_type=F32)
    out = out + jnp.dot(s, mid, preferred_element_type=F32)
    return out + jnp.dot(s, hi, preferred_element_type=F32)


def _dot3(a, b):
    a_hi = a.astype(BF16)
    a_lo = (a - a_hi.astype(F32)).astype(BF16)
    b_hi = b.astype(BF16)
    b_lo = (b - b_hi.astype(F32)).astype(BF16)
    out = jnp.dot(a_lo, b_hi, preferred_element_type=F32)
    out = out + jnp.dot(a_hi, b_lo, preferred_element_type=F32)
    return out + jnp.dot(a_hi, b_hi, preferred_element_type=F32)


def _sigmoid(x):
    return 1.0 / (1.0 + jnp.exp(-x))


def _softplus(x):
    return jnp.maximum(x, 0.0) + jnp.log(1.0 + jnp.exp(-jnp.abs(x)))


def _gelu_tanh(x):
    c = math.sqrt(2.0 / math.pi)
    return 0.5 * x * (1.0 + jnp.tanh(c * (x + 0.044715 * (x * x * x))))


def _layer_norm(x, g, b):
    mu = jnp.mean(x, axis=-1, keepdims=True)
    xc = x - mu
    var = jnp.mean(xc * xc, axis=-1, keepdims=True)
    return xc * lax.rsqrt(var + LN_EPS) * g + b


def _iota2(shape, dim):
    return lax.broadcasted_iota(jnp.int32, shape, dim)


def _head_ones():
    r = _iota2((GROUP_W, GROUP_W), 0) // HEAD_DIM
    c = _iota2((GROUP_W, GROUP_W), 1) // HEAD_DIM
    return (r == c).astype(F32)


def _stack_heads(x):
    lane_head = _iota2(x.shape, 1) // HEAD_DIM
    return jnp.concatenate([jnp.where(lane_head == h, x, 0.0) for h in range(N_HEADS)], axis=0)


def _unstack_heads(x_sw, rows):
    out = x_sw[0:rows]
    for h in range(1, N_HEADS):
        out = out + x_sw[h * rows:(h + 1) * rows]
    return out


def _shift_rows(x, prev, n):
    ext = jnp.concatenate([prev, x], axis=0)
    return ext[8 - n:8 - n + x.shape[0]]


def _full_spec(shape):
    nd = len(shape)
    return pl.BlockSpec(shape, lambda *_: (0,) * nd)


def _params(sem):
    return pltpu.CompilerParams(dimension_semantics=sem, vmem_limit_bytes=VMEM_LIMIT)


def _ln_kernel(x_ref, g_ref, b_ref, o_ref):
    o_ref[...] = _layer_norm(x_ref[...], g_ref[...], b_ref[...])


def _ln_call(x2d, g, b, rows):
    n, d = x2d.shape
    return pl.pallas_call(
        _ln_kernel,
        grid=(n // rows,),
        in_specs=[pl.BlockSpec((rows, d), lambda i: (i, 0)), _full_spec((1, d)), _full_spec((1, d))],
        out_specs=pl.BlockSpec((rows, d), lambda i: (i, 0)),
        out_shape=jax.ShapeDtypeStruct((n, d), F32),
        compiler_params=_params(("parallel",)),
        name="ln_in",
    )(x2d, g.reshape(1, d), b.reshape(1, d))


def _inproj_kernel(h_ref, *refs):
    nw = len(refs) // 2
    hb = h_ref[...].astype(BF16)
    for w_ref, o_ref in zip(refs[:nw], refs[nw:]):
        o_ref[...] = jnp.dot(hb, w_ref[...], preferred_element_type=F32)


def _inproj_call(h2d, weights, rows):
    n, d = h2d.shape
    return pl.pallas_call(
        _inproj_kernel,
        grid=(n // rows,),
        in_specs=[pl.BlockSpec((rows, d), lambda i: (i, 0))] + [_full_spec(w.shape) for w in weights],
        out_specs=[pl.BlockSpec((rows, w.shape[1]), lambda i: (i, 0)) for w in weights],
        out_shape=[jax.ShapeDtypeStruct((n, w.shape[1]), F32) for w in weights],
        compiler_params=_params(("parallel",)),
        name="in_proj",
    )(h2d, *weights)


def _s5_kernel(u_ref, lre_ref, lim_ref, dt_ref, bmat_ref, cmat_ref, d_ref, wglu_ref, bglu_ref,
               y_ref, bu_scr, s_scr, st_scr, *, steps):
    @pl.when(pl.program_id(0) == 0)
    def _():
        st_scr[...] = jnp.zeros_like(st_scr)

    lr, li, dt = lre_ref[...], lim_ref[...], jnp.exp(dt_ref[...])
    mag = jnp.exp(lr * dt)
    ab_re, ab_im = mag * jnp.cos(li * dt), mag * jnp.sin(li * dt)
    den = lr * lr + li * li
    z_re = ((ab_re - 1.0) * lr + ab_im * li) / den
    z_im = (ab_im * lr - (ab_re - 1.0) * li) / den

    u = u_ref[...]
    p = _bdot(u, bmat_ref[...])
    p_re, p_im = p[:, :S5_STATE], p[:, S5_STATE:]
    bu_scr[:, :S5_STATE] = z_re * p_re - z_im * p_im
    bu_scr[:, S5_STATE:] = z_re * p_im + z_im * p_re

    a_re = jnp.broadcast_to(ab_re, (8, S5_STATE))
    a_im = jnp.broadcast_to(ab_im, (8, S5_STATE))

    def step(t, carry):
        s_re, s_im = carry
        row = pl.multiple_of(t * 8, 8)
        n_re = a_re * s_re - a_im * s_im + bu_scr[pl.ds(row, 8), :S5_STATE]
        n_im = a_re * s_im + a_im * s_re + bu_scr[pl.ds(row, 8), S5_STATE:]
        s_scr[pl.ds(row, 8), :S5_STATE] = n_re
        s_scr[pl.ds(row, 8), S5_STATE:] = n_im
        return n_re, n_im

    s_re, s_im = lax.fori_loop(0, steps, step, (st_scr[:, :S5_STATE], st_scr[:, S5_STATE:]))
    st_scr[:, :S5_STATE] = s_re
    st_scr[:, S5_STATE:] = s_im

    y = _bdot(s_scr[...], cmat_ref[...]) + d_ref[...] * u
    y = _gelu_tanh(y)
    y_ref[...] = y * _sigmoid(_bdot(y, wglu_ref[...]) + bglu_ref[...])


def _s5_call(u_tm, lam_re, lam_im, log_dt, b_re, b_im, c_re, c_im, d, w_glu, b_glu, steps):
    n = u_tm.shape[0]
    gp = S5_STATE
    eye_g = jnp.eye(S5_G, dtype=F32)
    b_bd = lambda b: jnp.einsum("gpc,gh->gchp", b, eye_g).reshape(GROUP_W, gp)
    c_bd = lambda c: jnp.einsum("gcp,gh->gphc", c, eye_g).reshape(gp, GROUP_W)
    bmat = jnp.concatenate([b_bd(b_re), b_bd(b_im)], axis=1).astype(BF16)
    cmat = jnp.concatenate([c_bd(c_re), -c_bd(c_im)], axis=0).astype(BF16)
    dt = jnp.repeat(log_dt, S5_P).reshape(1, gp)
    rows = steps * 8
    kern = functools.partial(_s5_kernel, steps=steps)
    return pl.pallas_call(
        kern,
        grid=(n // rows,),
        in_specs=[pl.BlockSpec((rows, GROUP_W), lambda i: (i, 0)),
                  _full_spec((1, gp)), _full_spec((1, gp)), _full_spec((1, gp)),
                  _full_spec((GROUP_W, 2 * gp)), _full_spec((2 * gp, GROUP_W)),
                  _full_spec((1, GROUP_W)), _full_spec((GROUP_W, GROUP_W)), _full_spec((1, GROUP_W))],
        out_specs=pl.BlockSpec((rows, GROUP_W), lambda i: (i, 0)),
        out_shape=jax.ShapeDtypeStruct((n, GROUP_W), F32),
        scratch_shapes=[pltpu.VMEM((rows, 2 * gp), F32), pltpu.VMEM((rows, 2 * gp), F32),
                        pltpu.VMEM((8, 2 * gp), F32)],
        compiler_params=_params(("arbitrary",)),
        name="s5_mixer",
    )(u_tm, lam_re.reshape(1, gp), lam_im.reshape(1, gp), dt, bmat, cmat,
      d.reshape(1, GROUP_W), w_glu.astype(BF16), b_glu.reshape(1, GROUP_W))


def _unit_lower_inverse_apply(a, rhs):
    n = a.shape[0]
    ri, ci = _iota2((n, n), 0), _iota2((n, n), 1)
    eye = (ri == ci).astype(F32)
    a_d = jnp.where((ri // 16) == (ci // 16), a, 0.0)
    a_o = a - a_d
    x = eye - a_d
    p = _dot3(a_d, a_d)
    x = x + _dot3(x, p)
    p = _dot3(p, p)
    x = x + _dot3(x, p)
    p = _dot3(p, p)
    d_inv = x + _dot3(x, p)
    nn = _dot3(d_inv, a_o)
    w = eye - nn
    w = w + _dot3(w, _dot3(nn, nn))
    return _dot3(w, _dot3(d_inv, rhs))


def _rwkv_kernel(z_ref, mu_ref, w0_ref, w2_ref, a0_ref, a2_ref, g2_ref, kk_ref, ka_ref, rk_ref,
                 lng_ref, lnb_ref, y_ref, prev_scr, n_scr):
    L = CHUNK

    @pl.when(pl.program_id(1) == 0)
    def _():
        prev_scr[...] = jnp.zeros_like(prev_scr)
        n_scr[...] = jnp.zeros_like(n_scr)

    z = z_ref[...]
    zs = _shift_rows(z, prev_scr[...], 1)
    prev_scr[...] = z[L - 8:L]
    zm = z + (zs - z) * mu_ref[...]
    r, k, v = zm[:, 0:256], zm[:, 256:512], zm[:, 512:768]
    slab, xg = zm[:, 768:768 + RW_LORA], zm[:, 896:1024]

    w_raw = -_softplus(-(w0_ref[...] + _bdot(jnp.tanh(slab), w2_ref[...]))) - 0.5
    lw = -jnp.exp(w_raw)
    a = _sigmoid(a0_ref[...] + _bdot(slab, a2_ref[...]))
    g = _bdot(_sigmoid(xg), g2_ref[...])

    hones = _head_ones()
    kk = k * kk_ref[...]
    kkn = kk / jnp.maximum(jnp.sqrt(_xdot(kk * kk, hones)), 1e-12)
    kt = k * (1.0 + (a - 1.0) * ka_ref[...])
    beta = kkn * a

    tri = (_iota2((L, L), 0) >= _iota2((L, L), 1)).astype(F32)
    cum = _xdot_left(tri, lw)
    cum_last = cum[L - 1:L, :]
    e_neg = jnp.exp(-cum)
    e_tail = jnp.exp(cum_last - cum)
    kq_s = _stack_heads(kkn * jnp.exp(cum - lw))
    rq_s = _stack_heads(r * jnp.exp(cum))
    kk_s = _stack_heads(kt * e_neg)
    kb_s = _stack_heads(beta * e_neg)
    v_s = _stack_heads(v)

    n4 = N_HEADS * L
    ti, si = _iota2((n4, n4), 0) % L, _iota2((n4, n4), 1) % L
    strict, incl = ti > si, ti >= si
    a_b = jnp.where(strict, _bdot_nt(kq_s, kb_s), 0.0)
    a_k = jnp.where(strict, _bdot_nt(kq_s, kk_s), 0.0)
    b_k = jnp.where(incl, _bdot_nt(rq_s, kk_s), 0.0)
    b_b = jnp.where(incl, _bdot_nt(rq_s, kb_s), 0.0)

    n0 = n_scr[...]
    rhs = _bdot_nt(kq_s, n0) + _bdot(a_k, v_s)
    u_sw = _unit_lower_inverse_apply(a_b, rhs)
    y_sw = _bdot_nt(rq_s, n0) + _bdot(b_k, v_s) - _bdot(b_b, u_sw)
    y = _unstack_heads(y_sw, L)
    u_w = _unstack_heads(u_sw, L)

    same_head = (_iota2((GROUP_W, GROUP_W), 0) // HEAD_DIM) == (_iota2((GROUP_W, GROUP_W), 1) // HEAD_DIM)
    upd = _bdot_tn(v, kt * e_tail) - _bdot_tn(u_w, beta * e_tail)
    n_scr[...] = n0 * jnp.exp(cum_last) + jnp.where(same_head, upd, 0.0)

    inv_hd = 1.0 / HEAD_DIM
    mean = _xdot(y, hones) * inv_hd
    yc = y - mean
    var = _xdot(yc * yc, hones) * inv_hd
    yn = yc * lax.rsqrt(var + RW_GN_EPS) * lng_ref[...] + lnb_ref[...]
    bonus = _xdot(r * kt * rk_ref[...], hones)
    y_ref[...] = (yn + bonus * v) * g


def _rwkv_call(z_rw, mu, w0, w2, a0, a2, g2, k_k, k_a, r_k, ln_g, ln_b):
    bsz, seq, width = z_rw.shape
    row = lambda t: t.reshape(1, -1)
    zeros = jnp.zeros((RW_LORA // 2, GROUP_W), F32)
    w2p = jnp.concatenate([w2, zeros], axis=0).astype(BF16)
    a2p = jnp.concatenate([zeros, a2], axis=0).astype(BF16)
    vec = _full_spec((1, GROUP_W))
    mat = _full_spec((RW_LORA, GROUP_W))
    return pl.pallas_call(
        _rwkv_kernel,
        grid=(bsz, seq // CHUNK),
        in_specs=[pl.BlockSpec((None, CHUNK, width), lambda b, i: (b, i, 0)),
                  _full_spec((1, width)), vec, mat, vec, mat, mat, vec, vec, vec, vec, vec],
        out_specs=pl.BlockSpec((None, CHUNK, GROUP_W), lambda b, i: (b, i, 0)),
        out_shape=jax.ShapeDtypeStruct((bsz, seq, GROUP_W), F32),
        scratch_shapes=[pltpu.VMEM((8, width), F32), pltpu.VMEM((GROUP_W, GROUP_W), F32)],
        compiler_params=_params(("arbitrary", "arbitrary")),
        name="rwkv7_mixer",
    )(z_rw, row(mu), row(w0), w2p, row(a0), a2p, g2.astype(BF16), row(k_k), row(k_a), row(r_k),
      row(ln_g), row(ln_b))


def _sb_kernel(q_ref, k_ref, v_ref, o_ref, *, blk):
    qi = pl.program_id(1)
    q = q_ref[...] * (HEAD_DIM ** -0.5)
    lane_head = _iota2((blk, GROUP_W), 1) // HEAD_DIM
    row_pos = qi * blk + _iota2((blk, blk), 0)
    col_in = _iota2((blk, blk), 1)
    later = (_iota2((blk, blk), 0) > col_in).astype(F32)

    out = jnp.zeros((blk, GROUP_W), F32)
    for h in range(N_HEADS):
        qh = jnp.where(lane_head == h, q, 0.0).astype(BF16)

        def cond(state):
            kb, carry, _ = state
            return jnp.logical_and(kb >= 0, jnp.max(carry) > -SB_SKIP_LOG)

        def body(state):
            kb, carry, acc = state
            start = pl.multiple_of(kb * blk, blk)
            kblk = k_ref[pl.ds(start, blk), :]
            vblk = v_ref[pl.ds(start, blk), :]
            logits = _bdot_nt(qh, kblk)
            sp = _softplus(logits)
            mask = (kb * blk + col_in) < row_pos
            log_stay = jnp.where(mask, -sp, 0.0)
            after = _xdot(log_stay, later) + carry
            wgt = jnp.where(mask, jnp.exp(logits - sp + after), 0.0)
            acc = acc + _bdot(wgt, vblk)
            carry = carry + jnp.sum(log_stay, axis=1, keepdims=True)
            return kb - 1, carry, acc

        init = (qi, jnp.zeros((blk, 1), F32), jnp.zeros((blk, GROUP_W), F32))
        _, _, acc = lax.while_loop(cond, body, init)
        out = out + jnp.where(lane_head == h, acc, 0.0)
    o_ref[...] = out


def _sb_call(z_sb, blk=128):
    bsz, seq, _ = z_sb.shape
    kern = functools.partial(_sb_kernel, blk=blk)
    return pl.pallas_call(
        kern,
        grid=(bsz, seq // blk),
        in_specs=[pl.BlockSpec((None, blk, GROUP_W), lambda b, i: (b, i, 0)),
                  pl.BlockSpec((None, seq, GROUP_W), lambda b, i: (b, 0, 1)),
                  pl.BlockSpec((None, seq, GROUP_W), lambda b, i: (b, 0, 2))],
        out_specs=pl.BlockSpec((None, blk, GROUP_W), lambda b, i: (b, i, 0)),
        out_shape=jax.ShapeDtypeStruct((bsz, seq, GROUP_W), F32),
        compiler_params=_params(("parallel", "arbitrary")),
        name="stickbreak_attn",
    )(z_sb, z_sb, z_sb)


def _mlstm_kernel(z_ref, g_ref, cw_ref, cb_ref, bi_ref, bf_ref, lng_ref, y_ref,
                  prev_scr, c_scr, n_scr, m_scr):
    L = CHUNK

    @pl.when(pl.program_id(1) == 0)
    def _():
        prev_scr[...] = jnp.zeros_like(prev_scr)
        c_scr[...] = jnp.zeros_like(c_scr)
        n_scr[...] = jnp.zeros_like(n_scr)
        m_scr[...] = jnp.full(m_scr.shape, -jnp.inf, F32)

    z = z_ref[...]
    qk_raw = z[:, 0:2 * GROUP_W]
    prev = prev_scr[...]
    prev_scr[...] = qk_raw[L - 8:L]
    cw = cw_ref[...]
    conv = qk_raw * cw[3:4] + cb_ref[...]
    for j in range(3):
        conv = conv + _shift_rows(qk_raw, prev, 3 - j) * cw[j:j + 1]
    qk = conv * _sigmoid(conv)
    q, k = qk[:, :GROUP_W], qk[:, GROUP_W:] * (HEAD_DIM ** -0.5)
    v, o = z[:, 2 * GROUP_W:3 * GROUP_W], z[:, 3 * GROUP_W:]

    gz = g_ref[...]
    gate_col = _iota2((gz.shape[1], GROUP_W), 0)
    gate_head = _iota2((gz.shape[1], GROUP_W), 1) // HEAD_DIM
    ig = _xdot(gz, (gate_col == gate_head).astype(F32)) + bi_ref[...]
    fg = _xdot(gz, (gate_col == gate_head + N_HEADS).astype(F32)) + bf_ref[...]

    tri = (_iota2((L, L), 0) >= _iota2((L, L), 1)).astype(F32)
    b = _xdot_left(tri, -_softplus(-fg))
    b_last = b[L - 1:L, :]
    c_prev, n_prev, m_prev = c_scr[...], n_scr[0:1, :], m_scr[0:1, :]

    gl = b_last - b + ig
    m_loc = jnp.max(gl, axis=0, keepdims=True)
    kw = k * jnp.exp(gl - m_loc)
    same_head = (_iota2((GROUP_W, GROUP_W), 0) // HEAD_DIM) == (_iota2((GROUP_W, GROUP_W), 1) // HEAD_DIM)
    c_loc = jnp.where(same_head, _bdot_tn(kw, v), 0.0)
    n_loc = jnp.sum(kw, axis=0, keepdims=True)
    m_new = jnp.maximum(b_last + m_prev, m_loc)
    s_old, s_new = jnp.exp(b_last + m_prev - m_new), jnp.exp(m_loc - m_new)
    c_scr[...] = s_old * c_prev + s_new * c_loc
    n_scr[...] = jnp.broadcast_to(s_old * n_prev + s_new * n_loc, n_scr.shape)
    m_scr[...] = jnp.broadcast_to(m_new, m_scr.shape)

    lane = _iota2((L, GROUP_W), 1)
    lane_head, lane_s = lane // HEAD_DIM, lane % HEAD_DIM
    row_l = _iota2((L, GROUP_W), 0)
    x = ig - b
    x_row = _xdot_left(jnp.ones((L, L), F32), jnp.where(row_l == lane_s, x, 0.0))
    log_d = b + x_row
    causal = lane_s <= row_l
    m_intra = jnp.full((L, GROUP_W), -jnp.inf, F32)
    for h in range(N_HEADS):
        sel = lane_head == h
        mh = jnp.max(jnp.where(jnp.logical_and(sel, causal), log_d, -jnp.inf), axis=1, keepdims=True)
        m_intra = jnp.where(sel, mh, m_intra)
    log_inter = b + m_prev
    m_t = jnp.maximum(log_inter, m_intra)
    w_intra = jnp.where(causal, jnp.exp(log_d - m_t), 0.0)
    w_inter = jnp.exp(log_inter - m_t)

    hones = _head_ones()
    s_mat = _bdot_nt(q, _stack_heads(k)) * w_intra
    num = w_inter * _bdot(q, c_prev) + _bdot(s_mat, _stack_heads(v))
    den = w_inter * _xdot(q * n_prev, hones) + _bdot(s_mat, hones)
    hval = num / jnp.maximum(jnp.abs(den), jnp.exp(-m_t))

    inv_hd = 1.0 / HEAD_DIM
    mean = _xdot(hval, hones) * inv_hd
    hc = hval - mean
    var = _xdot(hc * hc, hones) * inv_hd
    y_ref[...] = _sigmoid(o) * (hc * lax.rsqrt(var + LN_EPS) * lng_ref[...])


def _mlstm_call(z_ml, z_gate, conv_w, conv_b, b_i, b_f, ln_g):
    bsz, seq, width = z_ml.shape
    gw = z_gate.shape[2]
    lanes = lambda t: jnp.repeat(t, HEAD_DIM).reshape(1, GROUP_W)
    vec = _full_spec((1, GROUP_W))
    return pl.pallas_call(
        _mlstm_kernel,
        grid=(bsz, seq // CHUNK),
        in_specs=[pl.BlockSpec((None, CHUNK, width), lambda b, i: (b, i, 0)),
                  pl.BlockSpec((None, CHUNK, gw), lambda b, i: (b, i, 0)),
                  _full_spec(conv_w.shape), _full_spec((1, 2 * GROUP_W)), vec, vec, vec],
        out_specs=pl.BlockSpec((None, CHUNK, GROUP_W), lambda b, i: (b, i, 0)),
        out_shape=jax.ShapeDtypeStruct((bsz, seq, GROUP_W), F32),
        scratch_shapes=[pltpu.VMEM((8, 2 * GROUP_W), F32), pltpu.VMEM((GROUP_W, GROUP_W), F32),
                        pltpu.VMEM((8, GROUP_W), F32), pltpu.VMEM((8, GROUP_W), F32)],
        compiler_params=_params(("arbitrary", "arbitrary")),
        name="mlstm_mixer",
    )(z_ml, z_gate, conv_w, conv_b.reshape(1, -1), lanes(b_i), lanes(b_f), ln_g.reshape(1, -1))


def _resident_spec(shape):
    nd = len(shape)
    return pl.BlockSpec(shape, lambda *_: (0,) * nd, pipeline_mode=pl.Buffered(1))


def _outproj_kernel(h_ref, y0_ref, y1_ref, y2_ref, y3_ref, w_ref, g_ref, b_ref, o_ref, *, alpha):
    acc = alpha * h_ref[...]
    for i, y_ref in enumerate((y0_ref, y1_ref, y2_ref, y3_ref)):
        acc = acc + jnp.dot(y_ref[...].astype(BF16), w_ref[i * GROUP_W:(i + 1) * GROUP_W, :],
                            preferred_element_type=F32)
    o_ref[...] = _layer_norm(acc, g_ref[...], b_ref[...])


def _outproj_call(h2d, ys, w_out, g, b, alpha, rows):
    n, d = h2d.shape
    row_spec = lambda w: pl.BlockSpec((rows, w), lambda i: (i, 0))
    return pl.pallas_call(
        functools.partial(_outproj_kernel, alpha=alpha),
        grid=(n // rows,),
        in_specs=[row_spec(d)] + [row_spec(GROUP_W)] * 4 + [_resident_spec(w_out.shape),
                                                            _full_spec((1, d)), _full_spec((1, d))],
        out_specs=row_spec(d),
        out_shape=jax.ShapeDtypeStruct((n, d), F32),
        compiler_params=_params(("parallel",)),
        name="out_proj_ln",
    )(h2d, *ys, w_out.astype(BF16), g.reshape(1, d), b.reshape(1, d))


def _ffn_kernel(h_ref, wup_ref, cw_ref, cb_ref, wdn_ref, g_ref, b_ref, o_ref, prev_scr, *, alpha, d_ff):
    @pl.when(pl.program_id(1) == 0)
    def _():
        prev_scr[...] = jnp.zeros_like(prev_scr)

    h = h_ref[...]
    rows = h.shape[0]
    up = jnp.dot(h.astype(BF16), wup_ref[...], preferred_element_type=F32)
    u, gate = up[:, :d_ff], up[:, d_ff:]
    prev = prev_scr[...]
    prev_scr[...] = u[rows - 8:rows]
    cw = cw_ref[...]
    conv = u * cw[2:3] + cb_ref[...]
    conv = conv + _shift_rows(u, prev, 1) * cw[1:2] + _shift_rows(u, prev, 2) * cw[0:1]
    act = (_gelu_tanh(conv) * gate).astype(BF16)
    down = jnp.dot(act, wdn_ref[...], preferred_element_type=F32)
    o_ref[...] = _layer_norm(alpha * h + down, g_ref[...], b_ref[...])


def _ffn_call(h, w_up, conv_w, conv_b, w_down, g, b, alpha, rows=256):
    bsz, seq, d = h.shape
    d_ff = w_down.shape[0]
    blk = pl.BlockSpec((None, rows, d), lambda bi, i: (bi, i, 0))
    return pl.pallas_call(
        functools.partial(_ffn_kernel, alpha=alpha, d_ff=d_ff),
        grid=(bsz, seq // rows),
        in_specs=[blk, _resident_spec(w_up.shape), _full_spec(conv_w.shape), _full_spec((1, d_ff)),
                  _resident_spec(w_down.shape), _full_spec((1, d)), _full_spec((1, d))],
        out_specs=blk,
        out_shape=jax.ShapeDtypeStruct((bsz, seq, d), F32),
        scratch_shapes=[pltpu.VMEM((8, d_ff), F32)],
        compiler_params=_params(("arbitrary", "arbitrary")),
        name="conv_ffn_ln",
    )(h, w_up.astype(BF16), conv_w, conv_b.reshape(1, d_ff), w_down.astype(BF16),
      g.reshape(1, d), b.reshape(1, d))


DEPTH = 2
DN_ALPHA = (2 * DEPTH) ** 0.25
IN_SIZES = (256, 1024, 768, 1024, 8)
GATE_PAD = 128
S5_STEPS = 128
ROW_TILE = 512


def kernel(x, ln_in_g, ln_in_b, w_in, s5_lambda_re, s5_lambda_im, s5_log_dt, s5_b_re, s5_b_im, s5_c_re, s5_c_im, s5_d, s5_w_glu, s5_b_glu, rw_mu, rw_w0, rw_w2, rw_a0, rw_a2, rw_g2, rw_k_k, rw_k_a, rw_r_k, rw_ln_g, rw_ln_b, ml_conv_w, ml_conv_b, ml_b_i, ml_b_f, ml_ln_g, w_out, ln1_g, ln1_b, ffn_w_up, ffn_conv_w, ffn_conv_b, ffn_w_down, ln2_g, ln2_b):
    bsz, seq, d = x.shape
    n = bsz * seq
    h = _ln_call(x.reshape(n, d), ln_in_g, ln_in_b, ROW_TILE)
    cuts = [0]
    for s in IN_SIZES:
        cuts.append(cuts[-1] + s)
    for l in range(DEPTH):
        ws = [w_in[l][:, cuts[i]:cuts[i + 1]].astype(BF16) for i in range(len(IN_SIZES))]
        ws[4] = jnp.pad(ws[4], ((0, 0), (0, GATE_PAD - IN_SIZES[4])))
        z_s5, z_rw, z_sb, z_ml, z_g = _inproj_call(h, ws, ROW_TILE)

        u_tm = jnp.transpose(z_s5.reshape(bsz, seq, GROUP_W), (1, 0, 2)).reshape(n, GROUP_W)
        y_s5 = _s5_call(u_tm, s5_lambda_re[l], s5_lambda_im[l], s5_log_dt[l], s5_b_re[l], s5_b_im[l],
                        s5_c_re[l], s5_c_im[l], s5_d[l], s5_w_glu[l], s5_b_glu[l], S5_STEPS)
        y_s5 = jnp.transpose(y_s5.reshape(seq, bsz, GROUP_W), (1, 0, 2)).reshape(n, GROUP_W)
        y_rw = _rwkv_call(z_rw.reshape(bsz, seq, -1), rw_mu[l], rw_w0[l], rw_w2[l], rw_a0[l], rw_a2[l],
                          rw_g2[l], rw_k_k[l], rw_k_a[l], rw_r_k[l], rw_ln_g[l], rw_ln_b[l])
        y_sb = _sb_call(z_sb.reshape(bsz, seq, -1))
        y_ml = _mlstm_call(z_ml.reshape(bsz, seq, -1), z_g.reshape(bsz, seq, -1), ml_conv_w[l],
                           ml_conv_b[l], ml_b_i[l], ml_b_f[l], ml_ln_g[l])
        ys = [y_s5, y_rw.reshape(n, GROUP_W), y_sb.reshape(n, GROUP_W), y_ml.reshape(n, GROUP_W)]
        h = _outproj_call(h, ys, w_out[l], ln1_g[l], ln1_b[l], DN_ALPHA, ROW_TILE)
        h = _ffn_call(h.reshape(bsz, seq, d), ffn_w_up[l], ffn_conv_w[l], ffn_conv_b[l], ffn_w_down[l],
                      ln2_g[l], ln2_b[l], DN_ALPHA).reshape(n, d)
    return h.reshape(bsz, seq, d)
```

```python
import functools
import math

import jax
import jax.numpy as jnp
from jax import lax
from jax.experimental import pallas as pl
from jax.experimental.pallas import tpu as pltpu

F32 = jnp.float32
BF16 = jnp.bfloat16

GROUP_W = 256
HEAD_DIM = 64
N_HEADS = GROUP_W // HEAD_DIM
CHUNK = 64
S5_G, S5_P, S5_CH = 16, 64, 16
S5_STATE = S5_G * S5_P
RW_LORA = 128
LN_EPS = 1e-5
RW_GN_EPS = 64e-5
SB_SKIP_LOG = 110.0
VMEM_LIMIT = 56 * 1024 * 1024


def _bdot(a, b):
    return jnp.dot(a.astype(BF16), b.astype(BF16), preferred_element_type=F32)


def _bdot_nt(a, b):
    return lax.dot_general(a.astype(BF16), b.astype(BF16), (((1,), (1,)), ((), ())),
                           preferred_element_type=F32)


def _bdot_tn(a, b):
    return lax.dot_general(a.astype(BF16), b.astype(BF16), (((0,), (0,)), ((), ())),
                           preferred_element_type=F32)


def _split3(x):
    hi = x.astype(BF16)
    r1 = x - hi.astype(F32)
    mid = r1.astype(BF16)
    lo = (r1 - mid.astype(F32)).astype(BF16)
    return hi, mid, lo


def _xdot(a, sel):
    hi, mid, lo = _split3(a)
    s = sel.astype(BF16)
    out = jnp.dot(lo, s, preferred_element_type=F32)
    out = out + jnp.dot(mid, s, preferred_element_type=F32)
    return out + jnp.dot(hi, s, preferred_element_type=F32)


def _xdot_left(sel, a):
    hi, mid, lo = _split3(a)
    s = sel.astype(BF16)
    out = jnp.dot(s, lo, preferred_element_type=F32)
    out = out + jnp.dot(s, mid, preferred_element_type=F32)
    return out + jnp.dot(s, hi, preferred_element_type=F32)


def _split2(x):
    hi = x.astype(BF16)
    return hi, (x - hi.astype(F32)).astype(BF16)


def _sdot(a, sel):
    hi, lo = _split2(a)
    s = sel.astype(BF16)
    return jnp.dot(lo, s, preferred_element_type=F32) + jnp.dot(hi, s, preferred_element_type=F32)


def _sdot_left(sel, a):
    hi, lo = _split2(a)
    s = sel.astype(BF16)
    return jnp.dot(s, lo, preferred_element_type=F32) + jnp.dot(s, hi, preferred_element_type=F32)


def _sigmoid(x):
    return 1.0 / (1.0 + jnp.exp(-x))


def _softplus(x):
    return jnp.maximum(x, 0.0) + jnp.log(1.0 + jnp.exp(-jnp.abs(x)))


def _gelu_tanh(x):
    c = math.sqrt(2.0 / math.pi)
    return 0.5 * x * (1.0 + jnp.tanh(c * (x + 0.044715 * (x * x * x))))


def _layer_norm(x, g, b):
    mu = jnp.mean(x, axis=-1, keepdims=True)
    xc = x - mu
    var = jnp.mean(xc * xc, axis=-1, keepdims=True)
    return xc * lax.rsqrt(var + LN_EPS) * g + b


def _iota2(shape, dim):
    return lax.broadcasted_iota(jnp.int32, shape, dim)


def _head_ones():
    r = _iota2((GROUP_W, GROUP_W), 0) // HEAD_DIM
    c = _iota2((GROUP_W, GROUP_W), 1) // HEAD_DIM
    return (r == c).astype(F32)


def _stack_heads(x):
    lane_head = _iota2(x.shape, 1) // HEAD_DIM
    return jnp.concatenate([jnp.where(lane_head == h, x, 0.0) for h in range(N_HEADS)], axis=0)


def _unstack_heads(x_sw, rows):
    out = x_sw[0:rows]
    for h in range(1, N_HEADS):
        out = out + x_sw[h * rows:(h + 1) * rows]
    return out


def _shift_rows(x, prev, n):
    ext = jnp.concatenate([prev, x], axis=0)
    return ext[8 - n:8 - n + x.shape[0]]


def _full_spec(shape):
    nd = len(shape)
    return pl.BlockSpec(shape, lambda *_: (0,) * nd)


def _params(sem):
    return pltpu.CompilerParams(dimension_semantics=sem, vmem_limit_bytes=VMEM_LIMIT)


def _ln_kernel(x_ref, g_ref, b_ref, o_ref):
    o_ref[...] = _layer_norm(x_ref[...], g_ref[...], b_ref[...])


def _ln_call(x2d, g, b, rows):
    n, d = x2d.shape
    return pl.pallas_call(
        _ln_kernel,
        grid=(n // rows,),
        in_specs=[pl.BlockSpec((rows, d), lambda i: (i, 0)), _full_spec((1, d)), _full_spec((1, d))],
        out_specs=pl.BlockSpec((rows, d), lambda i: (i, 0)),
        out_shape=jax.ShapeDtypeStruct((n, d), F32),
        compiler_params=_params(("parallel",)),
        name="ln_in",
    )(x2d, g.reshape(1, d), b.reshape(1, d))


def _inproj_kernel(h_ref, *refs):
    nw = len(refs) // 2
    hb = h_ref[...].astype(BF16)
    for w_ref, o_ref in zip(refs[:nw], refs[nw:]):
        o_ref[...] = jnp.dot(hb, w_ref[...], preferred_element_type=F32)


def _inproj_call(h2d, weights, rows):
    n, d = h2d.shape
    return pl.pallas_call(
        _inproj_kernel,
        grid=(n // rows,),
        in_specs=[pl.BlockSpec((rows, d), lambda i: (i, 0))] + [_full_spec(w.shape) for w in weights],
        out_specs=[pl.BlockSpec((rows, w.shape[1]), lambda i: (i, 0)) for w in weights],
        out_shape=[jax.ShapeDtypeStruct((n, w.shape[1]), F32) for w in weights],
        compiler_params=_params(("parallel",)),
        name="in_proj",
    )(h2d, *weights)


def _s5_kernel(u_ref, lre_ref, lim_ref, dt_ref, bmat_ref, cmat_ref, d_ref, wglu_ref, bglu_ref,
               y_ref, bu_scr, s_scr, st_scr, *, steps):
    @pl.when(pl.program_id(0) == 0)
    def _():
        st_scr[...] = jnp.zeros_like(st_scr)

    lr, li, dt = lre_ref[...], lim_ref[...], jnp.exp(dt_ref[...])
    mag = jnp.exp(lr * dt)
    ab_re, ab_im = mag * jnp.cos(li * dt), mag * jnp.sin(li * dt)
    den = lr * lr + li * li
    z_re = ((ab_re - 1.0) * lr + ab_im * li) / den
    z_im = (ab_im * lr - (ab_re - 1.0) * li) / den

    u = u_ref[...]
    p = _bdot(u, bmat_ref[...])
    p_re, p_im = p[:, :S5_STATE], p[:, S5_STATE:]
    bu_scr[:, :S5_STATE] = z_re * p_re - z_im * p_im
    bu_scr[:, S5_STATE:] = z_re * p_im + z_im * p_re

    a_re = jnp.broadcast_to(ab_re, (8, S5_STATE))
    a_im = jnp.broadcast_to(ab_im, (8, S5_STATE))

    def step(t, carry):
        s_re, s_im = carry
        row = pl.multiple_of(t * 8, 8)
        n_re = a_re * s_re - a_im * s_im + bu_scr[pl.ds(row, 8), :S5_STATE]
        n_im = a_re * s_im + a_im * s_re + bu_scr[pl.ds(row, 8), S5_STATE:]
        s_scr[pl.ds(row, 8), :S5_STATE] = n_re
        s_scr[pl.ds(row, 8), S5_STATE:] = n_im
        return n_re, n_im

    s_re, s_im = lax.fori_loop(0, steps, step, (st_scr[:, :S5_STATE], st_scr[:, S5_STATE:]))
    st_scr[:, :S5_STATE] = s_re
    st_scr[:, S5_STATE:] = s_im

    y = _bdot(s_scr[...], cmat_ref[...]) + d_ref[...] * u
    y = _gelu_tanh(y)
    y_ref[...] = y * _sigmoid(_bdot(y, wglu_ref[...]) + bglu_ref[...])


def _s5_call(u_tm, lam_re, lam_im, log_dt, b_re, b_im, c_re, c_im, d, w_glu, b_glu, steps):
    n = u_tm.shape[0]
    gp = S5_STATE
    eye_g = jnp.eye(S5_G, dtype=F32)
    b_bd = lambda b: jnp.einsum("gpc,gh->gchp", b, eye_g).reshape(GROUP_W, gp)
    c_bd = lambda c: jnp.einsum("gcp,gh->gphc", c, eye_g).reshape(gp, GROUP_W)
    bmat = jnp.concatenate([b_bd(b_re), b_bd(b_im)], axis=1).astype(BF16)
    cmat = jnp.concatenate([c_bd(c_re), -c_bd(c_im)], axis=0).astype(BF16)
    dt = jnp.repeat(log_dt, S5_P).reshape(1, gp)
    rows = steps * 8
    kern = functools.partial(_s5_kernel, steps=steps)
    return pl.pallas_call(
        kern,
        grid=(n // rows,),
        in_specs=[pl.BlockSpec((rows, GROUP_W), lambda i: (i, 0)),
                  _full_spec((1, gp)), _full_spec((1, gp)), _full_spec((1, gp)),
                  _full_spec((GROUP_W, 2 * gp)), _full_spec((2 * gp, GROUP_W)),
                  _full_spec((1, GROUP_W)), _full_spec((GROUP_W, GROUP_W)), _full_spec((1, GROUP_W))],
        out_specs=pl.BlockSpec((rows, GROUP_W), lambda i: (i, 0)),
        out_shape=jax.ShapeDtypeStruct((n, GROUP_W), F32),
        scratch_shapes=[pltpu.VMEM((rows, 2 * gp), F32), pltpu.VMEM((rows, 2 * gp), F32),
                        pltpu.VMEM((8, 2 * gp), F32)],
        compiler_params=_params(("arbitrary",)),
        name="s5_mixer",
    )(u_tm, lam_re.reshape(1, gp), lam_im.reshape(1, gp), dt, bmat, cmat,
      d.reshape(1, GROUP_W), w_glu.astype(BF16), b_glu.reshape(1, GROUP_W))


def _unit_lower_inverse_apply(a, rhs):
    n = a.shape[0]
    ri, ci = _iota2((n, n), 0), _iota2((n, n), 1)
    eye = (ri == ci).astype(F32)
    a_d = jnp.where((ri // 16) == (ci // 16), a, 0.0)
    a_o = a - a_d
    x = eye - a_d
    p = _bdot(a_d, a_d)
    x = x + _bdot(x, p)
    p = _bdot(p, p)
    x = x + _bdot(x, p)
    p = _bdot(p, p)
    d_inv = x + _bdot(x, p)
    nn = _bdot(d_inv, a_o)
    w = eye - nn
    w = w + _bdot(w, _bdot(nn, nn))
    return _bdot(w, _bdot(d_inv, rhs))


def _rwkv_kernel(z_ref, mu_ref, w0_ref, w2_ref, a0_ref, a2_ref, g2_ref, kk_ref, ka_ref, rk_ref,
                 lng_ref, lnb_ref, y_ref, prev_scr, n_scr):
    @pl.when(pl.program_id(1) == 0)
    def _():
        prev_scr[...] = jnp.zeros_like(prev_scr)
        n_scr[...] = jnp.zeros_like(n_scr)

    params = tuple(ref[...] for ref in (mu_ref, w0_ref, w2_ref, a0_ref, a2_ref, g2_ref, kk_ref, ka_ref,
                                        rk_ref, lng_ref, lnb_ref))
    for i in range(z_ref.shape[0]):
        y, n_new = _rwkv_chunk(z_ref[i], prev_scr[i], n_scr[i], params)
        prev_scr[i] = z_ref[i, CHUNK - 8:CHUNK, :]
        n_scr[i] = n_new
        y_ref[i] = y


def _rwkv_chunk(z, prev, n0, params):
    mu, w0, w2, a0, a2, g2, k_k, k_a, r_k, ln_g, ln_b = params
    L = CHUNK
    zs = _shift_rows(z, prev, 1)
    zm = z + (zs - z) * mu
    r, k, v = zm[:, 0:256], zm[:, 256:512], zm[:, 512:768]
    slab, xg = zm[:, 768:768 + RW_LORA], zm[:, 896:1024]

    w_raw = -_softplus(-(w0 + _bdot(jnp.tanh(slab), w2))) - 0.5
    lw = -jnp.exp(w_raw)
    a = _sigmoid(a0 + _bdot(slab, a2))
    g = _bdot(_sigmoid(xg), g2)

    hones = _head_ones()
    kk = k * k_k
    kt = k * (1.0 + (a - 1.0) * k_a)
    sums = _sdot(jnp.concatenate([kk * kk, r * kt * r_k], axis=0), hones)
    kkn = kk / jnp.maximum(jnp.sqrt(sums[:L]), 1e-12)
    bonus = sums[L:]
    beta = kkn * a

    tri = (_iota2((L, L), 0) >= _iota2((L, L), 1)).astype(F32)
    cum = _sdot_left(tri, lw)
    cum_last = cum[L - 1:L, :]
    e_neg = jnp.exp(-cum)
    e_tail = jnp.exp(cum_last - cum)
    kq_s = _stack_heads(kkn * jnp.exp(cum - lw))
    rq_s = _stack_heads(r * jnp.exp(cum))
    kk_s = _stack_heads(kt * e_neg)
    kb_s = _stack_heads(beta * e_neg)
    v_s = _stack_heads(v)

    n4 = N_HEADS * L
    ti, si = _iota2((n4, n4), 0) % L, _iota2((n4, n4), 1) % L
    strict, incl = ti > si, ti >= si
    a_b = jnp.where(strict, _bdot_nt(kq_s, kb_s), 0.0)
    a_k = jnp.where(strict, _bdot_nt(kq_s, kk_s), 0.0)
    b_k = jnp.where(incl, _bdot_nt(rq_s, kk_s), 0.0)
    b_b = jnp.where(incl, _bdot_nt(rq_s, kb_s), 0.0)

    rhs = _bdot_nt(kq_s, n0) + _bdot(a_k, v_s)
    u_sw = _unit_lower_inverse_apply(a_b, rhs)
    y_sw = _bdot_nt(rq_s, n0) + _bdot(b_k, v_s) - _bdot(b_b, u_sw)
    y = _unstack_heads(y_sw, L)
    u_w = _unstack_heads(u_sw, L)

    same_head = (_iota2((GROUP_W, GROUP_W), 0) // HEAD_DIM) == (_iota2((GROUP_W, GROUP_W), 1) // HEAD_DIM)
    upd = _bdot_tn(v, kt * e_tail) - _bdot_tn(u_w, beta * e_tail)
    n_new = n0 * jnp.exp(cum_last) + jnp.where(same_head, upd, 0.0)

    inv_hd = 1.0 / HEAD_DIM
    mean = _sdot(y, hones) * inv_hd
    yc = y - mean
    var = _sdot(yc * yc, hones) * inv_hd
    yn = yc * lax.rsqrt(var + RW_GN_EPS) * ln_g + ln_b
    return (yn + bonus * v) * g, n_new


def _rwkv_call(z_rw, mu, w0, w2, a0, a2, g2, k_k, k_a, r_k, ln_g, ln_b, nb=2):
    bsz, seq, width = z_rw.shape
    row = lambda t: t.reshape(1, -1)
    zeros = jnp.zeros((RW_LORA // 2, GROUP_W), F32)
    w2p = jnp.concatenate([w2, zeros], axis=0).astype(BF16)
    a2p = jnp.concatenate([zeros, a2], axis=0).astype(BF16)
    vec = _full_spec((1, GROUP_W))
    mat = _full_spec((RW_LORA, GROUP_W))
    return pl.pallas_call(
        _rwkv_kernel,
        grid=(bsz // nb, seq // CHUNK),
        in_specs=[pl.BlockSpec((nb, CHUNK, width), lambda b, i: (b, i, 0)),
                  _full_spec((1, width)), vec, mat, vec, mat, mat, vec, vec, vec, vec, vec],
        out_specs=pl.BlockSpec((nb, CHUNK, GROUP_W), lambda b, i: (b, i, 0)),
        out_shape=jax.ShapeDtypeStruct((bsz, seq, GROUP_W), F32),
        scratch_shapes=[pltpu.VMEM((nb, 8, width), F32), pltpu.VMEM((nb, GROUP_W, GROUP_W), F32)],
        compiler_params=_params(("arbitrary", "arbitrary")),
        name="rwkv7_mixer",
    )(z_rw, row(mu), row(w0), w2p, row(a0), a2p, g2.astype(BF16), row(k_k), row(k_a), row(r_k),
      row(ln_g), row(ln_b))


def _sb_kernel(q_ref, k_ref, v_ref, o_ref, *, blk):
    qi = pl.program_id(1)
    n4 = N_HEADS * blk
    q_s = _stack_heads(q_ref[...] * (HEAD_DIM ** -0.5)).astype(BF16)
    row_pos = qi * blk + _iota2((n4, blk), 0) % blk
    col_in = _iota2((n4, blk), 1)
    later = (_iota2((blk, blk), 0) > _iota2((blk, blk), 1)).astype(F32)

    def cond(state):
        kb, carry, _ = state
        return jnp.logical_and(kb >= 0, jnp.max(carry) > -SB_SKIP_LOG)

    def body(state):
        kb, carry, acc = state
        start = pl.multiple_of(kb * blk, blk)
        kblk = k_ref[pl.ds(start, blk), :]
        vblk = v_ref[pl.ds(start, blk), :]
        logits = _bdot_nt(q_s, kblk)
        sp = _softplus(logits)
        mask = (kb * blk + col_in) < row_pos
        log_stay = jnp.where(mask, -sp, 0.0)
        after = _sdot(log_stay, later) + carry
        wgt = jnp.where(mask, jnp.exp(logits - sp + after), 0.0)
        wgt_w = jnp.concatenate([wgt[h * blk:(h + 1) * blk] for h in range(N_HEADS)], axis=1)
        acc = acc + _bdot(wgt_w, _stack_heads(vblk))
        carry = carry + jnp.sum(log_stay, axis=1, keepdims=True)
        return kb - 1, carry, acc

    init = (qi, jnp.zeros((n4, 1), F32), jnp.zeros((blk, GROUP_W), F32))
    _, _, acc = lax.while_loop(cond, body, init)
    o_ref[...] = acc


def _sb_call(z_sb, blk=128):
    bsz, seq, _ = z_sb.shape
    kern = functools.partial(_sb_kernel, blk=blk)
    return pl.pallas_call(
        kern,
        grid=(bsz, seq // blk),
        in_specs=[pl.BlockSpec((None, blk, GROUP_W), lambda b, i: (b, i, 0)),
                  pl.BlockSpec((None, seq, GROUP_W), lambda b, i: (b, 0, 1)),
                  pl.BlockSpec((None, seq, GROUP_W), lambda b, i: (b, 0, 2))],
        out_specs=pl.BlockSpec((None, blk, GROUP_W), lambda b, i: (b, i, 0)),
        out_shape=jax.ShapeDtypeStruct((bsz, seq, GROUP_W), F32),
        compiler_params=_params(("parallel", "arbitrary")),
        name="stickbreak_attn",
    )(z_sb, z_sb, z_sb)


def _mlstm_kernel(z_ref, g_ref, cw_ref, cb_ref, bi_ref, bf_ref, lng_ref, y_ref,
                  prev_scr, c_scr, n_scr, m_scr):
    @pl.when(pl.program_id(1) == 0)
    def _():
        prev_scr[...] = jnp.zeros_like(prev_scr)
        c_scr[...] = jnp.zeros_like(c_scr)
        n_scr[...] = jnp.zeros_like(n_scr)
        m_scr[...] = jnp.full(m_scr.shape, -jnp.inf, F32)

    params = tuple(ref[...] for ref in (cw_ref, cb_ref, bi_ref, bf_ref, lng_ref))
    for i in range(z_ref.shape[0]):
        state = (c_scr[i], n_scr[i, 0:1, :], m_scr[i, 0:1, :])
        y, (c_new, n_new, m_new) = _mlstm_chunk(z_ref[i], g_ref[i], prev_scr[i], state, params)
        prev_scr[i] = z_ref[i, CHUNK - 8:CHUNK, 0:2 * GROUP_W]
        c_scr[i] = c_new
        n_scr[i] = jnp.broadcast_to(n_new, n_scr.shape[1:])
        m_scr[i] = jnp.broadcast_to(m_new, m_scr.shape[1:])
        y_ref[i] = y


def _mlstm_chunk(z, gz, prev, state, params):
    cw, cb, b_i, b_f, ln_g = params
    c_prev, n_prev, m_prev = state
    L = CHUNK
    qk_raw = z[:, 0:2 * GROUP_W]
    conv = qk_raw * cw[3:4] + cb
    for j in range(3):
        conv = conv + _shift_rows(qk_raw, prev, 3 - j) * cw[j:j + 1]
    qk = conv * _sigmoid(conv)
    q, k = qk[:, :GROUP_W], qk[:, GROUP_W:] * (HEAD_DIM ** -0.5)
    v, o = z[:, 2 * GROUP_W:3 * GROUP_W], z[:, 3 * GROUP_W:]

    gate_col = _iota2((gz.shape[1], GROUP_W), 0)
    gate_head = _iota2((gz.shape[1], GROUP_W), 1) // HEAD_DIM
    ig = _sdot(gz, (gate_col == gate_head).astype(F32)) + b_i
    fg = _sdot(gz, (gate_col == gate_head + N_HEADS).astype(F32)) + b_f

    tri = (_iota2((L, L), 0) >= _iota2((L, L), 1)).astype(F32)
    b = _xdot_left(tri, -_softplus(-fg))
    b_last = b[L - 1:L, :]

    gl = b_last - b + ig
    m_loc = jnp.max(gl, axis=0, keepdims=True)
    kw = k * jnp.exp(gl - m_loc)
    same_head = (_iota2((GROUP_W, GROUP_W), 0) // HEAD_DIM) == (_iota2((GROUP_W, GROUP_W), 1) // HEAD_DIM)
    c_loc = jnp.where(same_head, _bdot_tn(kw, v), 0.0)
    n_loc = jnp.sum(kw, axis=0, keepdims=True)
    m_new = jnp.maximum(b_last + m_prev, m_loc)
    s_old, s_new = jnp.exp(b_last + m_prev - m_new), jnp.exp(m_loc - m_new)
    new_state = (s_old * c_prev + s_new * c_loc, s_old * n_prev + s_new * n_loc, m_new)

    lane = _iota2((L, GROUP_W), 1)
    lane_head, lane_s = lane // HEAD_DIM, lane % HEAD_DIM
    row_l = _iota2((L, GROUP_W), 0)
    x = ig - b
    x_row = _xdot_left(jnp.ones((L, L), F32), jnp.where(row_l == lane_s, x, 0.0))
    log_d = b + x_row
    causal = lane_s <= row_l
    m_intra = jnp.full((L, GROUP_W), -jnp.inf, F32)
    for h in range(N_HEADS):
        sel = lane_head == h
        mh = jnp.max(jnp.where(jnp.logical_and(sel, causal), log_d, -jnp.inf), axis=1, keepdims=True)
        m_intra = jnp.where(sel, mh, m_intra)
    log_inter = b + m_prev
    m_t = jnp.maximum(log_inter, m_intra)
    w_intra = jnp.where(causal, jnp.exp(log_d - m_t), 0.0)
    w_inter = jnp.exp(log_inter - m_t)

    hones = _head_ones()
    s_mat = _bdot_nt(q, _stack_heads(k)) * w_intra
    num = w_inter * _bdot(q, c_prev) + _bdot(s_mat, _stack_heads(v))
    den = w_inter * _sdot(q * n_prev, hones) + _bdot(s_mat, hones)
    hval = num / jnp.maximum(jnp.abs(den), jnp.exp(-m_t))

    inv_hd = 1.0 / HEAD_DIM
    mean = _sdot(hval, hones) * inv_hd
    hc = hval - mean
    var = _sdot(hc * hc, hones) * inv_hd
    return _sigmoid(o) * (hc * lax.rsqrt(var + LN_EPS) * ln_g), new_state


def _mlstm_call(z_ml, z_gate, conv_w, conv_b, b_i, b_f, ln_g, nb=2):
    bsz, seq, width = z_ml.shape
    gw = z_gate.shape[2]
    lanes = lambda t: jnp.repeat(t, HEAD_DIM).reshape(1, GROUP_W)
    vec = _full_spec((1, GROUP_W))
    return pl.pallas_call(
        _mlstm_kernel,
        grid=(bsz // nb, seq // CHUNK),
        in_specs=[pl.BlockSpec((nb, CHUNK, width), lambda b, i: (b, i, 0)),
                  pl.BlockSpec((nb, CHUNK, gw), lambda b, i: (b, i, 0)),
                  _full_spec(conv_w.shape), _full_spec((1, 2 * GROUP_W)), vec, vec, vec],
        out_specs=pl.BlockSpec((nb, CHUNK, GROUP_W), lambda b, i: (b, i, 0)),
        out_shape=jax.ShapeDtypeStruct((bsz, seq, GROUP_W), F32),
        scratch_shapes=[pltpu.VMEM((nb, 8, 2 * GROUP_W), F32), pltpu.VMEM((nb, GROUP_W, GROUP_W), F32),
                        pltpu.VMEM((nb, 8, GROUP_W), F32), pltpu.VMEM((nb, 8, GROUP_W), F32)],
        compiler_params=_params(("arbitrary", "arbitrary")),
        name="mlstm_mixer",
    )(z_ml, z_gate, conv_w, conv_b.reshape(1, -1), lanes(b_i), lanes(b_f), ln_g.reshape(1, -1))


def _resident_spec(shape):
    nd = len(shape)
    return pl.BlockSpec(shape, lambda *_: (0,) * nd, pipeline_mode=pl.Buffered(1))


def _outproj_kernel(h_ref, y0_ref, y1_ref, y2_ref, y3_ref, w_ref, g_ref, b_ref, o_ref, *, alpha):
    acc = alpha * h_ref[...]
    for i, y_ref in enumerate((y0_ref, y1_ref, y2_ref, y3_ref)):
        acc = acc + jnp.dot(y_ref[...].astype(BF16), w_ref[i * GROUP_W:(i + 1) * GROUP_W, :],
                            preferred_element_type=F32)
    o_ref[...] = _layer_norm(acc, g_ref[...], b_ref[...])


def _outproj_call(h2d, ys, w_out, g, b, alpha, rows):
    n, d = h2d.shape
    row_spec = lambda w: pl.BlockSpec((rows, w), lambda i: (i, 0))
    return pl.pallas_call(
        functools.partial(_outproj_kernel, alpha=alpha),
        grid=(n // rows,),
        in_specs=[row_spec(d)] + [row_spec(GROUP_W)] * 4 + [_resident_spec(w_out.shape),
                                                            _full_spec((1, d)), _full_spec((1, d))],
        out_specs=row_spec(d),
        out_shape=jax.ShapeDtypeStruct((n, d), F32),
        compiler_params=_params(("parallel",)),
        name="out_proj_ln",
    )(h2d, *ys, w_out.astype(BF16), g.reshape(1, d), b.reshape(1, d))


def _ffn_kernel(h_ref, wup_ref, cw_ref, cb_ref, wdn_ref, g_ref, b_ref, o_ref, prev_scr, *, alpha, d_ff):
    @pl.when(pl.program_id(1) == 0)
    def _():
        prev_scr[...] = jnp.zeros_like(prev_scr)

    h = h_ref[...]
    rows = h.shape[0]
    up = jnp.dot(h.astype(BF16), wup_ref[...], preferred_element_type=F32)
    u, gate = up[:, :d_ff], up[:, d_ff:]
    prev = prev_scr[...]
    prev_scr[...] = u[rows - 8:rows]
    cw = cw_ref[...]
    conv = u * cw[2:3] + cb_ref[...]
    conv = conv + _shift_rows(u, prev, 1) * cw[1:2] + _shift_rows(u, prev, 2) * cw[0:1]
    act = (_gelu_tanh(conv) * gate).astype(BF16)
    down = jnp.dot(act, wdn_ref[...], preferred_element_type=F32)
    o_ref[...] = _layer_norm(alpha * h + down, g_ref[...], b_ref[...])


def _ffn_call(h, w_up, conv_w, conv_b, w_down, g, b, alpha, rows=256):
    bsz, seq, d = h.shape
    d_ff = w_down.shape[0]
    blk = pl.BlockSpec((None, rows, d), lambda bi, i: (bi, i, 0))
    return pl.pallas_call(
        functools.partial(_ffn_kernel, alpha=alpha, d_ff=d_ff),
        grid=(bsz, seq // rows),
        in_specs=[blk, _resident_spec(w_up.shape), _full_spec(conv_w.shape), _full_spec((1, d_ff)),
                  _resident_spec(w_down.shape), _full_spec((1, d)), _full_spec((1, d))],
        out_specs=blk,
        out_shape=jax.ShapeDtypeStruct((bsz, seq, d), F32),
        scratch_shapes=[pltpu.VMEM((8, d_ff), F32)],
        compiler_params=_params(("arbitrary", "arbitrary")),
        name="conv_ffn_ln",
    )(h, w_up.astype(BF16), conv_w, conv_b.reshape(1, d_ff), w_down.astype(BF16),
      g.reshape(1, d), b.reshape(1, d))


DEPTH = 2
DN_ALPHA = (2 * DEPTH) ** 0.25
IN_SIZES = (256, 1024, 768, 1024, 8)
GATE_PAD = 128
S5_STEPS = 128
ROW_TILE = 512


def kernel(x, ln_in_g, ln_in_b, w_in, s5_lambda_re, s5_lambda_im, s5_log_dt, s5_b_re, s5_b_im, s5_c_re, s5_c_im, s5_d, s5_w_glu, s5_b_glu, rw_mu, rw_w0, rw_w2, rw_a0, rw_a2, rw_g2, rw_k_k, rw_k_a, rw_r_k, rw_ln_g, rw_ln_b, ml_conv_w, ml_conv_b, ml_b_i, ml_b_f, ml_ln_g, w_out, ln1_g, ln1_b, ffn_w_up, ffn_conv_w, ffn_conv_b, ffn_w_down, ln2_g, ln2_b):
    bsz, seq, d = x.shape
    n = bsz * seq
    h = _ln_call(x.reshape(n, d), ln_in_g, ln_in_b, ROW_TILE)
    cuts = [0]
    for s in IN_SIZES:
        cuts.append(cuts[-1] + s)
    for l in range(DEPTH):
        ws = [w_in[l][:, cuts[i]:cuts[i + 1]].astype(BF16) for i in range(len(IN_SIZES))]
        ws[4] = jnp.pad(ws[4], ((0, 0), (0, GATE_PAD - IN_SIZES[4])))
        z_s5, z_rw, z_sb, z_ml, z_g = _inproj_call(h, ws, ROW_TILE)

        u_tm = jnp.transpose(z_s5.reshape(bsz, seq, GROUP_W), (1, 0, 2)).reshape(n, GROUP_W)
        y_s5 = _s5_call(u_tm, s5_lambda_re[l], s5_lambda_im[l], s5_log_dt[l], s5_b_re[l], s5_b_im[l],
                        s5_c_re[l], s5_c_im[l], s5_d[l], s5_w_glu[l], s5_b_glu[l], S5_STEPS)
        y_s5 = jnp.transpose(y_s5.reshape(seq, bsz, GROUP_W), (1, 0, 2)).reshape(n, GROUP_W)
        y_rw = _rwkv_call(z_rw.reshape(bsz, seq, -1), rw_mu[l], rw_w0[l], rw_w2[l], rw_a0[l], rw_a2[l],
                          rw_g2[l], rw_k_k[l], rw_k_a[l], rw_r_k[l], rw_ln_g[l], rw_ln_b[l])
        y_sb = _sb_call(z_sb.reshape(bsz, seq, -1))
        y_ml = _mlstm_call(z_ml.reshape(bsz, seq, -1), z_g.reshape(bsz, seq, -1), ml_conv_w[l],
                           ml_conv_b[l], ml_b_i[l], ml_b_f[l], ml_ln_g[l])
        ys = [y_s5, y_rw.reshape(n, GROUP_W), y_sb.reshape(n, GROUP_W), y_ml.reshape(n, GROUP_W)]
        h = _outproj_call(h, ys, w_out[l], ln1_g[l], ln1_b[l], DN_ALPHA, ROW_TILE)
        h = _ffn_call(h.reshape(bsz, seq, d), ffn_w_up[l], ffn_conv_w[l], ffn_conv_b[l], ffn_w_down[l],
                      ln2_g[l], ln2_b[l], DN_ALPHA).reshape(n, d)
    return h.reshape(bsz, seq, d)
```

```python
import functools
import math

import jax
import jax.numpy as jnp
from jax import lax
from jax.experimental import pallas as pl
from jax.experimental.pallas import tpu as pltpu

F32 = jnp.float32
BF16 = jnp.bfloat16

GROUP_W = 256
HEAD_DIM = 64
N_HEADS = GROUP_W // HEAD_DIM
CHUNK = 64
S5_G, S5_P, S5_CH = 16, 64, 16
S5_STATE = S5_G * S5_P
RW_LORA = 128
LN_EPS = 1e-5
RW_GN_EPS = 64e-5
SB_SKIP_LOG = 110.0
VMEM_LIMIT = 56 * 1024 * 1024


def _bdot(a, b):
    return jnp.dot(a.astype(BF16), b.astype(BF16), preferred_element_type=F32)


def _bdot_nt(a, b):
    return lax.dot_general(a.astype(BF16), b.astype(BF16), (((1,), (1,)), ((), ())),
                           preferred_element_type=F32)


def _bdot_tn(a, b):
    return lax.dot_general(a.astype(BF16), b.astype(BF16), (((0,), (0,)), ((), ())),
                           preferred_element_type=F32)


def _split3(x):
    hi = x.astype(BF16)
    r1 = x - hi.astype(F32)
    mid = r1.astype(BF16)
    lo = (r1 - mid.astype(F32)).astype(BF16)
    return hi, mid, lo


def _xdot_left(sel, a):
    hi, mid, lo = _split3(a)
    s = sel.astype(BF16)
    out = jnp.dot(s, lo, preferred_element_type=F32)
    out = out + jnp.dot(s, mid, preferred_element_type=F32)
    return out + jnp.dot(s, hi, preferred_element_type=F32)


def _split2(x):
    hi = x.astype(BF16)
    return hi, (x - hi.astype(F32)).astype(BF16)


def _sdot(a, sel):
    hi, lo = _split2(a)
    s = sel.astype(BF16)
    return jnp.dot(lo, s, preferred_element_type=F32) + jnp.dot(hi, s, preferred_element_type=F32)


def _sdot_left(sel, a):
    hi, lo = _split2(a)
    s = sel.astype(BF16)
    return jnp.dot(s, lo, preferred_element_type=F32) + jnp.dot(s, hi, preferred_element_type=F32)


def _sigmoid(x):
    return 1.0 / (1.0 + jnp.exp(-x))


def _softplus(x):
    return jnp.maximum(x, 0.0) + jnp.log(1.0 + jnp.exp(-jnp.abs(x)))


def _gelu_tanh(x):
    c = math.sqrt(2.0 / math.pi)
    return 0.5 * x * (1.0 + jnp.tanh(c * (x + 0.044715 * (x * x * x))))


def _layer_norm(x, g, b):
    mu = jnp.mean(x, axis=-1, keepdims=True)
    xc = x - mu
    var = jnp.mean(xc * xc, axis=-1, keepdims=True)
    return xc * lax.rsqrt(var + LN_EPS) * g + b


def _iota2(shape, dim):
    return lax.broadcasted_iota(jnp.int32, shape, dim)


def _head_ones():
    r = _iota2((GROUP_W, GROUP_W), 0) // HEAD_DIM
    c = _iota2((GROUP_W, GROUP_W), 1) // HEAD_DIM
    return (r == c).astype(F32)


def _stack_heads(x):
    lane_head = _iota2(x.shape, 1) // HEAD_DIM
    return jnp.concatenate([jnp.where(lane_head == h, x, 0.0) for h in range(N_HEADS)], axis=0)


def _shift_rows(x, prev, n):
    ext = jnp.concatenate([prev, x], axis=0)
    return ext[8 - n:8 - n + x.shape[0]]


def _full_spec(shape):
    nd = len(shape)
    return pl.BlockSpec(shape, lambda *_: (0,) * nd)


def _params(sem):
    return pltpu.CompilerParams(dimension_semantics=sem, vmem_limit_bytes=VMEM_LIMIT)


def _inproj_kernel(h_ref, *refs):
    nw = len(refs) // 2
    hb = h_ref[...].astype(BF16)
    for w_ref, o_ref in zip(refs[:nw], refs[nw:]):
        o_ref[...] = jnp.dot(hb, w_ref[...], preferred_element_type=F32)


def _ln_inproj_kernel(x_ref, g_ref, b_ref, *refs):
    h = _layer_norm(x_ref[...], g_ref[...], b_ref[...])
    refs[len(refs) // 2][...] = h
    hb = h.astype(BF16)
    nw = len(refs) // 2
    for w_ref, o_ref in zip(refs[:nw], refs[nw + 1:]):
        o_ref[...] = jnp.dot(hb, w_ref[...], preferred_element_type=F32)


def _time_major_spec(rows, seq):
    tiles = seq // rows
    return pl.BlockSpec((rows, GROUP_W), lambda i: (i % tiles, i // tiles))


def _inproj_call(h2d, weights, rows, bsz, ln=None):
    n, d = h2d.shape
    seq = n // bsz
    row_spec = lambda w: pl.BlockSpec((rows, w), lambda i: (i, 0))
    out_specs = [row_spec(w.shape[1]) for w in weights]
    out_shape = [jax.ShapeDtypeStruct((n, w.shape[1]), F32) for w in weights]
    out_specs[0] = _time_major_spec(rows, seq)
    out_shape[0] = jax.ShapeDtypeStruct((seq, bsz * GROUP_W), F32)
    in_specs = [row_spec(d)]
    args = [h2d]
    if ln is not None:
        in_specs += [_full_spec((1, d)), _full_spec((1, d))]
        args += [ln[0].reshape(1, d), ln[1].reshape(1, d)]
        out_specs = [row_spec(d)] + out_specs
        out_shape = [jax.ShapeDtypeStruct((n, d), F32)] + out_shape
    return pl.pallas_call(
        _inproj_kernel if ln is None else _ln_inproj_kernel,
        grid=(n // rows,),
        in_specs=in_specs + [_full_spec(w.shape) for w in weights],
        out_specs=out_specs,
        out_shape=out_shape,
        compiler_params=_params(("parallel",)),
        name="in_proj" if ln is None else "ln_in_proj",
    )(*args, *weights)


def _s5_kernel(u_ref, lre_ref, lim_ref, dt_ref, bmat_ref, cmat_ref, d_ref, wglu_ref, bglu_ref,
               y_ref, bu_scr, s_scr, st_scr, *, steps):
    @pl.when(pl.program_id(0) == 0)
    def _():
        st_scr[...] = jnp.zeros_like(st_scr)

    lr, li, dt = lre_ref[...], lim_ref[...], jnp.exp(dt_ref[...])
    mag = jnp.exp(lr * dt)
    ab_re, ab_im = mag * jnp.cos(li * dt), mag * jnp.sin(li * dt)
    den = lr * lr + li * li
    z_re = ((ab_re - 1.0) * lr + ab_im * li) / den
    z_im = (ab_im * lr - (ab_re - 1.0) * li) / den

    bsz = st_scr.shape[0]
    u = u_ref[...]
    p = _bdot(u, bmat_ref[...])
    p_re, p_im = p[:, :S5_STATE], p[:, S5_STATE:]
    bu_scr[:, :S5_STATE] = z_re * p_re - z_im * p_im
    bu_scr[:, S5_STATE:] = z_re * p_im + z_im * p_re

    a_re = jnp.broadcast_to(ab_re, (bsz, S5_STATE))
    a_im = jnp.broadcast_to(ab_im, (bsz, S5_STATE))

    def step(t, carry):
        s_re, s_im = carry
        row = pl.multiple_of(t * bsz, bsz)
        n_re = a_re * s_re - a_im * s_im + bu_scr[pl.ds(row, bsz), :S5_STATE]
        n_im = a_re * s_im + a_im * s_re + bu_scr[pl.ds(row, bsz), S5_STATE:]
        s_scr[pl.ds(row, bsz), :S5_STATE] = n_re
        s_scr[pl.ds(row, bsz), S5_STATE:] = n_im
        return n_re, n_im

    s_re, s_im = lax.fori_loop(0, steps, step, (st_scr[:, :S5_STATE], st_scr[:, S5_STATE:]))
    st_scr[:, :S5_STATE] = s_re
    st_scr[:, S5_STATE:] = s_im

    y = _bdot(s_scr[...], cmat_ref[...]) + d_ref[...] * u
    y = _gelu_tanh(y)
    y_ref[...] = y * _sigmoid(_bdot(y, wglu_ref[...]) + bglu_ref[...])


def _s5_call(u_tm, bsz, lam_re, lam_im, log_dt, b_re, b_im, c_re, c_im, d, w_glu, b_glu, steps):
    n = u_tm.shape[0]
    gp = S5_STATE
    eye_g = jnp.eye(S5_G, dtype=F32)
    b_bd = lambda b: jnp.einsum("gpc,gh->gchp", b, eye_g).reshape(GROUP_W, gp)
    c_bd = lambda c: jnp.einsum("gcp,gh->gphc", c, eye_g).reshape(gp, GROUP_W)
    bmat = jnp.concatenate([b_bd(b_re), b_bd(b_im)], axis=1).astype(BF16)
    cmat = jnp.concatenate([c_bd(c_re), -c_bd(c_im)], axis=0).astype(BF16)
    dt = jnp.repeat(log_dt, S5_P).reshape(1, gp)
    rows = steps * bsz
    kern = functools.partial(_s5_kernel, steps=steps)
    blk = pl.BlockSpec((rows, GROUP_W), lambda i: (i, 0))
    return pl.pallas_call(
        kern,
        grid=(n // rows,),
        in_specs=[blk,
                  _full_spec((1, gp)), _full_spec((1, gp)), _full_spec((1, gp)),
                  _full_spec((GROUP_W, 2 * gp)), _full_spec((2 * gp, GROUP_W)),
                  _full_spec((1, GROUP_W)), _full_spec((GROUP_W, GROUP_W)), _full_spec((1, GROUP_W))],
        out_specs=blk,
        out_shape=jax.ShapeDtypeStruct((n, GROUP_W), F32),
        scratch_shapes=[pltpu.VMEM((rows, 2 * gp), F32), pltpu.VMEM((rows, 2 * gp), F32),
                        pltpu.VMEM((bsz, 2 * gp), F32)],
        compiler_params=_params(("arbitrary",)),
        name="s5_mixer",
    )(u_tm, lam_re.reshape(1, gp), lam_im.reshape(1, gp), dt, bmat, cmat,
      d.reshape(1, GROUP_W), w_glu.astype(BF16), b_glu.reshape(1, GROUP_W))


def _hdot(a_c, b):
    return _bdot(a_c, _stack_heads(b))


def _unit_lower_inverse_apply(a, rhs):
    ti, si = _iota2(a.shape, 0), _iota2(a.shape, 1) % HEAD_DIM
    eye = (ti == si).astype(F32)
    a_d = jnp.where((ti // 16) == (si // 16), a, 0.0)
    a_o = a - a_d
    x = eye - a_d
    p = _hdot(a_d, a_d)
    x = x + _hdot(x, p)
    p = _hdot(p, p)
    x = x + _hdot(x, p)
    p = _hdot(p, p)
    d_inv = x + _hdot(x, p)
    nn = _hdot(d_inv, a_o)
    w = eye - nn
    w = w + _hdot(w, _hdot(nn, nn))
    return _hdot(w, _hdot(d_inv, rhs))


def _rwkv_kernel(z_ref, mu_ref, w0_ref, w2_ref, a0_ref, a2_ref, g2_ref, kk_ref, ka_ref, rk_ref,
                 lng_ref, lnb_ref, y_ref, prev_scr, n_scr):
    @pl.when(pl.program_id(1) == 0)
    def _():
        prev_scr[...] = jnp.zeros_like(prev_scr)
        n_scr[...] = jnp.zeros_like(n_scr)

    params = tuple(ref[...] for ref in (mu_ref, w0_ref, w2_ref, a0_ref, a2_ref, g2_ref, kk_ref, ka_ref,
                                        rk_ref, lng_ref, lnb_ref))
    chunk = jax.vmap(lambda z, prev, n0, tri: _rwkv_chunk(z, prev, n0, tri, params))
    y, n_new = chunk(z_ref[...], prev_scr[...], n_scr[...], _batched_tri(z_ref.shape[0]))
    prev_scr[...] = z_ref[:, CHUNK - 8:CHUNK, :]
    n_scr[...] = n_new
    y_ref[...] = y


def _batched_tri(nb):
    shape = (nb, CHUNK, CHUNK)
    return (_iota2(shape, 1) >= _iota2(shape, 2)).astype(F32)


def _rwkv_chunk(z, prev, n0, tri, params):
    mu, w0, w2, a0, a2, g2, k_k, k_a, r_k, ln_g, ln_b = params
    L = CHUNK
    zs = _shift_rows(z, prev, 1)
    zm = z + (zs - z) * mu
    r, k, v = zm[:, 0:256], zm[:, 256:512], zm[:, 512:768]
    slab, xg = zm[:, 768:768 + RW_LORA], zm[:, 896:1024]

    w_raw = -_softplus(-(w0 + _bdot(jnp.tanh(slab), w2))) - 0.5
    lw = -jnp.exp(w_raw)
    a = _sigmoid(a0 + _bdot(slab, a2))
    g = _bdot(_sigmoid(xg), g2)

    hones = _head_ones()
    kk = k * k_k
    kt = k * (1.0 + (a - 1.0) * k_a)
    sums = _sdot(jnp.concatenate([kk * kk, r * kt * r_k], axis=0), hones)
    kkn = kk / jnp.maximum(jnp.sqrt(sums[:L]), 1e-12)
    bonus = sums[L:]
    beta = kkn * a

    cum = _sdot_left(tri, lw)
    cum_last = cum[L - 1:L, :]
    e_neg = jnp.exp(-cum)
    e_tail = jnp.exp(cum_last - cum)
    qr = jnp.concatenate([kkn * jnp.exp(cum - lw), r * jnp.exp(cum)], axis=0)
    kkb_s = jnp.concatenate([_stack_heads(kt * e_neg), _stack_heads(beta * e_neg)], axis=0)
    prod = _bdot_nt(qr, kkb_s)
    ti, si = _iota2((L, GROUP_W), 0), _iota2((L, GROUP_W), 1) % HEAD_DIM
    strict, incl = ti > si, ti >= si
    a_k = jnp.where(strict, prod[:L, :GROUP_W], 0.0)
    a_b = jnp.where(strict, prod[:L, GROUP_W:], 0.0)
    b_k = jnp.where(incl, prod[L:, :GROUP_W], 0.0)
    b_b = jnp.where(incl, prod[L:, GROUP_W:], 0.0)

    reads = _bdot_nt(qr, n0)
    v_s = _stack_heads(v)
    u_w = _unit_lower_inverse_apply(a_b, reads[:L] + _bdot(a_k, v_s))
    y = reads[L:] + _bdot(b_k, v_s) - _hdot(b_b, u_w)

    same_head = (_iota2((GROUP_W, GROUP_W), 0) // HEAD_DIM) == (_iota2((GROUP_W, GROUP_W), 1) // HEAD_DIM)
    upd = _bdot_tn(jnp.concatenate([v, -u_w], axis=0),
                   jnp.concatenate([kt * e_tail, beta * e_tail], axis=0))
    n_new = n0 * jnp.exp(cum_last) + jnp.where(same_head, upd, 0.0)

    inv_hd = 1.0 / HEAD_DIM
    mean = _sdot(y, hones) * inv_hd
    yc = y - mean
    var = _sdot(yc * yc, hones) * inv_hd
    yn = yc * lax.rsqrt(var + RW_GN_EPS) * ln_g + ln_b
    return (yn + bonus * v) * g, n_new


def _rwkv_call(z_rw, mu, w0, w2, a0, a2, g2, k_k, k_a, r_k, ln_g, ln_b, nb=8):
    bsz, seq, width = z_rw.shape
    row = lambda t: t.reshape(1, -1)
    zeros = jnp.zeros((RW_LORA // 2, GROUP_W), F32)
    w2p = jnp.concatenate([w2, zeros], axis=0).astype(BF16)
    a2p = jnp.concatenate([zeros, a2], axis=0).astype(BF16)
    vec = _full_spec((1, GROUP_W))
    mat = _full_spec((RW_LORA, GROUP_W))
    return pl.pallas_call(
        _rwkv_kernel,
        grid=(bsz // nb, seq // CHUNK),
        in_specs=[pl.BlockSpec((nb, CHUNK, width), lambda b, i: (b, i, 0)),
                  _full_spec((1, width)), vec, mat, vec, mat, mat, vec, vec, vec, vec, vec],
        out_specs=pl.BlockSpec((nb, CHUNK, GROUP_W), lambda b, i: (b, i, 0)),
        out_shape=jax.ShapeDtypeStruct((bsz, seq, GROUP_W), F32),
        scratch_shapes=[pltpu.VMEM((nb, 8, width), F32), pltpu.VMEM((nb, GROUP_W, GROUP_W), F32)],
        compiler_params=_params(("arbitrary", "arbitrary")),
        name="rwkv7_mixer",
    )(z_rw, row(mu), row(w0), w2p, row(a0), a2p, g2.astype(BF16), row(k_k), row(k_a), row(r_k),
      row(ln_g), row(ln_b))


def _sb_kernel(q_ref, k_ref, v_ref, o_ref, *, blk):
    qi = pl.program_id(1)
    n4 = N_HEADS * blk
    q_s = _stack_heads(q_ref[...] * (HEAD_DIM ** -0.5)).astype(BF16)
    row_pos = qi * blk + _iota2((n4, blk), 0) % blk
    col_in = _iota2((n4, blk), 1)
    later = (_iota2((blk, blk), 0) > _iota2((blk, blk), 1)).astype(F32)

    def cond(state):
        kb, carry, _ = state
        return jnp.logical_and(kb >= 0, jnp.max(carry) > -SB_SKIP_LOG)

    def body(state):
        kb, carry, acc = state
        start = pl.multiple_of(kb * blk, blk)
        kblk = k_ref[pl.ds(start, blk), :]
        vblk = v_ref[pl.ds(start, blk), :]
        logits = _bdot_nt(q_s, kblk)
        sp = _softplus(logits)
        mask = (kb * blk + col_in) < row_pos
        log_stay = jnp.where(mask, -sp, 0.0)
        after = _sdot(log_stay, later) + carry
        wgt = jnp.where(mask, jnp.exp(logits - sp + after), 0.0)
        wgt_w = jnp.concatenate([wgt[h * blk:(h + 1) * blk] for h in range(N_HEADS)], axis=1)
        acc = acc + _bdot(wgt_w, _stack_heads(vblk))
        carry = carry + jnp.sum(log_stay, axis=1, keepdims=True)
        return kb - 1, carry, acc

    init = (qi, jnp.zeros((n4, 1), F32), jnp.zeros((blk, GROUP_W), F32))
    _, _, acc = lax.while_loop(cond, body, init)
    o_ref[...] = acc


def _sb_call(z_sb, blk=256):
    bsz, seq, _ = z_sb.shape
    kern = functools.partial(_sb_kernel, blk=blk)
    return pl.pallas_call(
        kern,
        grid=(bsz, seq // blk),
        in_specs=[pl.BlockSpec((None, blk, GROUP_W), lambda b, i: (b, i, 0)),
                  pl.BlockSpec((None, seq, GROUP_W), lambda b, i: (b, 0, 1)),
                  pl.BlockSpec((None, seq, GROUP_W), lambda b, i: (b, 0, 2))],
        out_specs=pl.BlockSpec((None, blk, GROUP_W), lambda b, i: (b, i, 0)),
        out_shape=jax.ShapeDtypeStruct((bsz, seq, GROUP_W), F32),
        compiler_params=_params(("parallel", "arbitrary")),
        name="stickbreak_attn",
    )(z_sb, z_sb, z_sb)


def _mlstm_kernel(z_ref, g_ref, cw_ref, cb_ref, bi_ref, bf_ref, lng_ref, y_ref,
                  prev_scr, c_scr, n_scr, m_scr):
    @pl.when(pl.program_id(1) == 0)
    def _():
        prev_scr[...] = jnp.zeros_like(prev_scr)
        c_scr[...] = jnp.zeros_like(c_scr)
        n_scr[...] = jnp.zeros_like(n_scr)
        m_scr[...] = jnp.full(m_scr.shape, -jnp.inf, F32)

    params = tuple(ref[...] for ref in (cw_ref, cb_ref, bi_ref, bf_ref, lng_ref))
    chunk = jax.vmap(lambda z, gz, prev, state, tri: _mlstm_chunk(z, gz, prev, state, tri, params))
    state = (c_scr[...], n_scr[:, 0:1, :], m_scr[:, 0:1, :])
    y, (c_new, n_new, m_new) = chunk(z_ref[...], g_ref[...], prev_scr[...], state,
                                     _batched_tri(z_ref.shape[0]))
    prev_scr[...] = z_ref[:, CHUNK - 8:CHUNK, 0:2 * GROUP_W]
    c_scr[...] = c_new
    n_scr[...] = jnp.broadcast_to(n_new, n_scr.shape)
    m_scr[...] = jnp.broadcast_to(m_new, m_scr.shape)
    y_ref[...] = y


def _mlstm_chunk(z, gz, prev, state, tri, params):
    cw, cb, b_i, b_f, ln_g = params
    c_prev, n_prev, m_prev = state
    L = CHUNK
    qk_raw = z[:, 0:2 * GROUP_W]
    conv = qk_raw * cw[3:4] + cb
    for j in range(3):
        conv = conv + _shift_rows(qk_raw, prev, 3 - j) * cw[j:j + 1]
    qk = conv * _sigmoid(conv)
    q, k = qk[:, :GROUP_W], qk[:, GROUP_W:] * (HEAD_DIM ** -0.5)
    v, o = z[:, 2 * GROUP_W:3 * GROUP_W], z[:, 3 * GROUP_W:]

    gate_col = _iota2((gz.shape[1], GROUP_W), 0)
    gate_head = _iota2((gz.shape[1], GROUP_W), 1) // HEAD_DIM
    ig = _sdot(gz, (gate_col == gate_head).astype(F32)) + b_i
    fg = _sdot(gz, (gate_col == gate_head + N_HEADS).astype(F32)) + b_f

    b = _xdot_left(tri, -_softplus(-fg))
    b_last = b[L - 1:L, :]

    gl = b_last - b + ig
    m_loc = jnp.max(gl, axis=0, keepdims=True)
    kw = k * jnp.exp(gl - m_loc)
    same_head = (_iota2((GROUP_W, GROUP_W), 0) // HEAD_DIM) == (_iota2((GROUP_W, GROUP_W), 1) // HEAD_DIM)
    c_loc = jnp.where(same_head, _bdot_tn(kw, v), 0.0)
    n_loc = jnp.sum(kw, axis=0, keepdims=True)
    m_new = jnp.maximum(b_last + m_prev, m_loc)
    s_old, s_new = jnp.exp(b_last + m_prev - m_new), jnp.exp(m_loc - m_new)
    new_state = (s_old * c_prev + s_new * c_loc, s_old * n_prev + s_new * n_loc, m_new)

    lane = _iota2((L, GROUP_W), 1)
    lane_head, lane_s = lane // HEAD_DIM, lane % HEAD_DIM
    row_l = _iota2((L, GROUP_W), 0)
    x = ig - b
    x_row = _xdot_left(jnp.maximum(tri, 1.0), jnp.where(row_l == lane_s, x, 0.0))
    log_d = b + x_row
    causal = lane_s <= row_l
    m_intra = jnp.full((L, GROUP_W), -jnp.inf, F32)
    for h in range(N_HEADS):
        sel = lane_head == h
        mh = jnp.max(jnp.where(jnp.logical_and(sel, causal), log_d, -jnp.inf), axis=1, keepdims=True)
        m_intra = jnp.where(sel, mh, m_intra)
    log_inter = b + m_prev
    m_t = jnp.maximum(log_inter, m_intra)
    w_intra = jnp.where(causal, jnp.exp(log_d - m_t), 0.0)
    w_inter = jnp.exp(log_inter - m_t)

    hones = _head_ones()
    s_mat = _bdot_nt(q, _stack_heads(k)) * w_intra
    num = w_inter * _bdot(q, c_prev) + _bdot(s_mat, _stack_heads(v))
    den = w_inter * _sdot(q * n_prev, hones) + _bdot(s_mat, hones)
    hval = num / jnp.maximum(jnp.abs(den), jnp.exp(-m_t))

    inv_hd = 1.0 / HEAD_DIM
    mean = _sdot(hval, hones) * inv_hd
    hc = hval - mean
    var = _sdot(hc * hc, hones) * inv_hd
    return _sigmoid(o) * (hc * lax.rsqrt(var + LN_EPS) * ln_g), new_state


def _mlstm_call(z_ml, z_gate, conv_w, conv_b, b_i, b_f, ln_g, nb=8):
    bsz, seq, width = z_ml.shape
    gw = z_gate.shape[2]
    lanes = lambda t: jnp.repeat(t, HEAD_DIM).reshape(1, GROUP_W)
    vec = _full_spec((1, GROUP_W))
    return pl.pallas_call(
        _mlstm_kernel,
        grid=(bsz // nb, seq // CHUNK),
        in_specs=[pl.BlockSpec((nb, CHUNK, width), lambda b, i: (b, i, 0)),
                  pl.BlockSpec((nb, CHUNK, gw), lambda b, i: (b, i, 0)),
                  _full_spec(conv_w.shape), _full_spec((1, 2 * GROUP_W)), vec, vec, vec],
        out_specs=pl.BlockSpec((nb, CHUNK, GROUP_W), lambda b, i: (b, i, 0)),
        out_shape=jax.ShapeDtypeStruct((bsz, seq, GROUP_W), F32),
        scratch_shapes=[pltpu.VMEM((nb, 8, 2 * GROUP_W), F32), pltpu.VMEM((nb, GROUP_W, GROUP_W), F32),
                        pltpu.VMEM((nb, 8, GROUP_W), F32), pltpu.VMEM((nb, 8, GROUP_W), F32)],
        compiler_params=_params(("arbitrary", "arbitrary")),
        name="mlstm_mixer",
    )(z_ml, z_gate, conv_w, conv_b.reshape(1, -1), lanes(b_i), lanes(b_f), ln_g.reshape(1, -1))


def _resident_spec(shape):
    nd = len(shape)
    return pl.BlockSpec(shape, lambda *_: (0,) * nd, pipeline_mode=pl.Buffered(1))


def _mix_ffn_kernel(h_ref, y0_ref, y1_ref, y2_ref, y3_ref, wout_ref, g1_ref, b1_ref,
                    wup_ref, cw_ref, cb_ref, wdn_ref, g_ref, b_ref, o_ref, prev_scr, *, alpha, d_ff):
    @pl.when(pl.program_id(1) == 0)
    def _():
        prev_scr[...] = jnp.zeros_like(prev_scr)

    acc = alpha * h_ref[...]
    for i, y_ref in enumerate((y0_ref, y1_ref, y2_ref, y3_ref)):
        acc = acc + jnp.dot(y_ref[...].astype(BF16), wout_ref[i * GROUP_W:(i + 1) * GROUP_W, :],
                            preferred_element_type=F32)
    h = _layer_norm(acc, g1_ref[...], b1_ref[...])

    rows = h.shape[0]
    up = jnp.dot(h.astype(BF16), wup_ref[...], preferred_element_type=F32)
    u, gate = up[:, :d_ff], up[:, d_ff:]
    prev = prev_scr[...]
    prev_scr[...] = u[rows - 8:rows]
    cw = cw_ref[...]
    conv = u * cw[2:3] + cb_ref[...]
    conv = conv + _shift_rows(u, prev, 1) * cw[1:2] + _shift_rows(u, prev, 2) * cw[0:1]
    act = (_gelu_tanh(conv) * gate).astype(BF16)
    down = jnp.dot(act, wdn_ref[...], preferred_element_type=F32)
    o_ref[...] = _layer_norm(alpha * h + down, g_ref[...], b_ref[...])


def _mix_ffn_call(h, ys, w_out, g1, b1, w_up, conv_w, conv_b, w_down, g2, b2, alpha, rows=256):
    bsz, seq, d = h.shape
    d_ff = w_down.shape[0]
    blk = lambda w: pl.BlockSpec((None, rows, w), lambda bi, i: (bi, i, 0))
    vec = _full_spec((1, d))
    return pl.pallas_call(
        functools.partial(_mix_ffn_kernel, alpha=alpha, d_ff=d_ff),
        grid=(bsz, seq // rows),
        in_specs=[blk(d), pl.BlockSpec((rows, GROUP_W), lambda bi, i: (i, bi))] + [blk(GROUP_W)] * 3
        + [_resident_spec(w_out.shape), vec, vec,
           _resident_spec(w_up.shape), _full_spec(conv_w.shape), _full_spec((1, d_ff)),
           _resident_spec(w_down.shape), vec, vec],
        out_specs=blk(d),
        out_shape=jax.ShapeDtypeStruct((bsz, seq, d), F32),
        scratch_shapes=[pltpu.VMEM((8, d_ff), F32)],
        compiler_params=_params(("arbitrary", "arbitrary")),
        name="mix_ffn_ln",
    )(h, *ys, w_out.astype(BF16), g1.reshape(1, d), b1.reshape(1, d),
      w_up.astype(BF16), conv_w, conv_b.reshape(1, d_ff), w_down.astype(BF16),
      g2.reshape(1, d), b2.reshape(1, d))


DEPTH = 2
DN_ALPHA = (2 * DEPTH) ** 0.25
IN_SIZES = (256, 1024, 768, 1024, 8)
GATE_PAD = 128
S5_STEPS = 128
ROW_TILE = 512


def kernel(x, ln_in_g, ln_in_b, w_in, s5_lambda_re, s5_lambda_im, s5_log_dt, s5_b_re, s5_b_im, s5_c_re, s5_c_im, s5_d, s5_w_glu, s5_b_glu, rw_mu, rw_w0, rw_w2, rw_a0, rw_a2, rw_g2, rw_k_k, rw_k_a, rw_r_k, rw_ln_g, rw_ln_b, ml_conv_w, ml_conv_b, ml_b_i, ml_b_f, ml_ln_g, w_out, ln1_g, ln1_b, ffn_w_up, ffn_conv_w, ffn_conv_b, ffn_w_down, ln2_g, ln2_b):
    bsz, seq, d = x.shape
    n = bsz * seq
    h = x.reshape(n, d)
    cuts = [0]
    for s in IN_SIZES:
        cuts.append(cuts[-1] + s)
    for l in range(DEPTH):
        ws = [w_in[l][:, cuts[i]:cuts[i + 1]].astype(BF16) for i in range(len(IN_SIZES))]
        ws[4] = jnp.pad(ws[4], ((0, 0), (0, GATE_PAD - IN_SIZES[4])))
        if l == 0:
            h, z_s5, z_rw, z_sb, z_ml, z_g = _inproj_call(h, ws, ROW_TILE, bsz, ln=(ln_in_g, ln_in_b))
        else:
            z_s5, z_rw, z_sb, z_ml, z_g = _inproj_call(h, ws, ROW_TILE, bsz)

        y_s5 = _s5_call(z_s5.reshape(n, GROUP_W), bsz, s5_lambda_re[l], s5_lambda_im[l], s5_log_dt[l],
                        s5_b_re[l], s5_b_im[l], s5_c_re[l], s5_c_im[l], s5_d[l], s5_w_glu[l], s5_b_glu[l],
                        S5_STEPS).reshape(seq, bsz * GROUP_W)
        y_rw = _rwkv_call(z_rw.reshape(bsz, seq, -1), rw_mu[l], rw_w0[l], rw_w2[l], rw_a0[l], rw_a2[l],
                          rw_g2[l], rw_k_k[l], rw_k_a[l], rw_r_k[l], rw_ln_g[l], rw_ln_b[l])
        y_sb = _sb_call(z_sb.reshape(bsz, seq, -1))
        y_ml = _mlstm_call(z_ml.reshape(bsz, seq, -1), z_g.reshape(bsz, seq, -1), ml_conv_w[l],
                           ml_conv_b[l], ml_b_i[l], ml_b_f[l], ml_ln_g[l])
        h = _mix_ffn_call(h.reshape(bsz, seq, d), [y_s5, y_rw, y_sb, y_ml], w_out[l], ln1_g[l], ln1_b[l],
                          ffn_w_up[l], ffn_conv_w[l], ffn_conv_b[l], ffn_w_down[l], ln2_g[l], ln2_b[l],
                          DN_ALPHA).reshape(n, d)
    return h.reshape(bsz, seq, d)
```

```python
import functools
import math

import jax
import jax.numpy as jnp
from jax import lax
from jax.experimental import pallas as pl
from jax.experimental.pallas import tpu as pltpu

F32 = jnp.float32
BF16 = jnp.bfloat16

GROUP_W = 256
HEAD_DIM = 64
N_HEADS = GROUP_W // HEAD_DIM
CHUNK = 64
S5_G, S5_P, S5_CH = 16, 64, 16
S5_STATE = S5_G * S5_P
RW_LORA = 128
LN_EPS = 1e-5
RW_GN_EPS = 64e-5
SB_SKIP_LOG = 110.0
VMEM_LIMIT = 56 * 1024 * 1024


def _bdot(a, b):
    return jnp.dot(a.astype(BF16), b.astype(BF16), preferred_element_type=F32)


def _bdot_nt(a, b):
    return lax.dot_general(a.astype(BF16), b.astype(BF16), (((1,), (1,)), ((), ())),
                           preferred_element_type=F32)


def _bdot_tn(a, b):
    return lax.dot_general(a.astype(BF16), b.astype(BF16), (((0,), (0,)), ((), ())),
                           preferred_element_type=F32)


def _split3(x):
    hi = x.astype(BF16)
    r1 = x - hi.astype(F32)
    mid = r1.astype(BF16)
    lo = (r1 - mid.astype(F32)).astype(BF16)
    return hi, mid, lo


def _xdot_left(sel, a):
    hi, mid, lo = _split3(a)
    s = sel.astype(BF16)
    out = jnp.dot(s, lo, preferred_element_type=F32)
    out = out + jnp.dot(s, mid, preferred_element_type=F32)
    return out + jnp.dot(s, hi, preferred_element_type=F32)


def _split2(x):
    hi = x.astype(BF16)
    return hi, (x - hi.astype(F32)).astype(BF16)


def _sdot(a, sel):
    hi, lo = _split2(a)
    s = sel.astype(BF16)
    return jnp.dot(lo, s, preferred_element_type=F32) + jnp.dot(hi, s, preferred_element_type=F32)


def _sdot_left(sel, a):
    hi, lo = _split2(a)
    s = sel.astype(BF16)
    return jnp.dot(s, lo, preferred_element_type=F32) + jnp.dot(s, hi, preferred_element_type=F32)


def _sigmoid(x):
    return 1.0 / (1.0 + jnp.exp(-x))


def _softplus(x):
    return jnp.maximum(x, 0.0) + jnp.log(1.0 + jnp.exp(-jnp.abs(x)))


def _gelu_tanh(x):
    c = math.sqrt(2.0 / math.pi)
    return 0.5 * x * (1.0 + jnp.tanh(c * (x + 0.044715 * (x * x * x))))


def _layer_norm(x, g, b):
    mu = jnp.mean(x, axis=-1, keepdims=True)
    xc = x - mu
    var = jnp.mean(xc * xc, axis=-1, keepdims=True)
    return xc * lax.rsqrt(var + LN_EPS) * g + b


def _iota2(shape, dim):
    return lax.broadcasted_iota(jnp.int32, shape, dim)


def _head_ones():
    r = _iota2((GROUP_W, GROUP_W), 0) // HEAD_DIM
    c = _iota2((GROUP_W, GROUP_W), 1) // HEAD_DIM
    return (r == c).astype(F32)


def _stack_heads(x):
    lane_head = _iota2(x.shape, 1) // HEAD_DIM
    return jnp.concatenate([jnp.where(lane_head == h, x, 0.0) for h in range(N_HEADS)], axis=0)


def _shift_rows(x, prev, n):
    ext = jnp.concatenate([prev, x], axis=0)
    return ext[8 - n:8 - n + x.shape[0]]


def _full_spec(shape):
    nd = len(shape)
    return pl.BlockSpec(shape, lambda *_: (0,) * nd)


def _params(sem):
    return pltpu.CompilerParams(dimension_semantics=sem, vmem_limit_bytes=VMEM_LIMIT)


def _inproj_kernel(h_ref, *refs):
    nw = len(refs) // 2
    hb = h_ref[...].astype(BF16)
    for w_ref, o_ref in zip(refs[:nw], refs[nw:]):
        o_ref[...] = jnp.dot(hb, w_ref[...], preferred_element_type=F32)


def _ln_inproj_kernel(x_ref, g_ref, b_ref, *refs):
    h = _layer_norm(x_ref[...], g_ref[...], b_ref[...])
    refs[len(refs) // 2][...] = h
    hb = h.astype(BF16)
    nw = len(refs) // 2
    for w_ref, o_ref in zip(refs[:nw], refs[nw + 1:]):
        o_ref[...] = jnp.dot(hb, w_ref[...], preferred_element_type=F32)


def _time_major_spec(rows, seq):
    tiles = seq // rows
    return pl.BlockSpec((rows, GROUP_W), lambda i: (i % tiles, i // tiles))


def _inproj_call(h2d, weights, rows, bsz, ln=None):
    n, d = h2d.shape
    seq = n // bsz
    row_spec = lambda w: pl.BlockSpec((rows, w), lambda i: (i, 0))
    out_specs = [row_spec(w.shape[1]) for w in weights]
    out_shape = [jax.ShapeDtypeStruct((n, w.shape[1]), F32) for w in weights]
    out_specs[0] = _time_major_spec(rows, seq)
    out_shape[0] = jax.ShapeDtypeStruct((seq, bsz * GROUP_W), F32)
    in_specs = [row_spec(d)]
    args = [h2d]
    if ln is not None:
        in_specs += [_full_spec((1, d)), _full_spec((1, d))]
        args += [ln[0].reshape(1, d), ln[1].reshape(1, d)]
        out_specs = [row_spec(d)] + out_specs
        out_shape = [jax.ShapeDtypeStruct((n, d), F32)] + out_shape
    return pl.pallas_call(
        _inproj_kernel if ln is None else _ln_inproj_kernel,
        grid=(n // rows,),
        in_specs=in_specs + [_full_spec(w.shape) for w in weights],
        out_specs=out_specs,
        out_shape=out_shape,
        compiler_params=_params(("parallel",)),
        name="in_proj" if ln is None else "ln_in_proj",
    )(*args, *weights)


def _s5_kernel(u_ref, lre_ref, lim_ref, dt_ref, bmat_ref, cmat_ref, d_ref, wglu_ref, bglu_ref,
               y_ref, bu_scr, s_scr, st_scr, *, steps):
    @pl.when(pl.program_id(0) == 0)
    def _():
        st_scr[...] = jnp.zeros_like(st_scr)

    lr, li, dt = lre_ref[...], lim_ref[...], jnp.exp(dt_ref[...])
    mag = jnp.exp(lr * dt)
    ab_re, ab_im = mag * jnp.cos(li * dt), mag * jnp.sin(li * dt)
    den = lr * lr + li * li
    z_re = ((ab_re - 1.0) * lr + ab_im * li) / den
    z_im = (ab_im * lr - (ab_re - 1.0) * li) / den

    bsz = st_scr.shape[0]
    u = u_ref[...]
    p = _bdot(u, bmat_ref[...])
    p_re, p_im = p[:, :S5_STATE], p[:, S5_STATE:]
    bu_scr[:, :S5_STATE] = z_re * p_re - z_im * p_im
    bu_scr[:, S5_STATE:] = z_re * p_im + z_im * p_re

    a_re = jnp.broadcast_to(ab_re, (bsz, S5_STATE))
    a_im = jnp.broadcast_to(ab_im, (bsz, S5_STATE))

    def step(t, carry):
        s_re, s_im = carry
        row = pl.multiple_of(t * bsz, bsz)
        n_re = a_re * s_re - a_im * s_im + bu_scr[pl.ds(row, bsz), :S5_STATE]
        n_im = a_re * s_im + a_im * s_re + bu_scr[pl.ds(row, bsz), S5_STATE:]
        s_scr[pl.ds(row, bsz), :S5_STATE] = n_re
        s_scr[pl.ds(row, bsz), S5_STATE:] = n_im
        return n_re, n_im

    s_re, s_im = lax.fori_loop(0, steps, step, (st_scr[:, :S5_STATE], st_scr[:, S5_STATE:]))
    st_scr[:, :S5_STATE] = s_re
    st_scr[:, S5_STATE:] = s_im

    y = _bdot(s_scr[...], cmat_ref[...]) + d_ref[...] * u
    y = _gelu_tanh(y)
    y_ref[...] = y * _sigmoid(_bdot(y, wglu_ref[...]) + bglu_ref[...])


def _s5_call(u_tm, bsz, lam_re, lam_im, log_dt, b_re, b_im, c_re, c_im, d, w_glu, b_glu, steps):
    n = u_tm.shape[0]
    gp = S5_STATE
    eye_g = jnp.eye(S5_G, dtype=F32)
    b_bd = lambda b: jnp.einsum("gpc,gh->gchp", b, eye_g).reshape(GROUP_W, gp)
    c_bd = lambda c: jnp.einsum("gcp,gh->gphc", c, eye_g).reshape(gp, GROUP_W)
    bmat = jnp.concatenate([b_bd(b_re), b_bd(b_im)], axis=1).astype(BF16)
    cmat = jnp.concatenate([c_bd(c_re), -c_bd(c_im)], axis=0).astype(BF16)
    dt = jnp.repeat(log_dt, S5_P).reshape(1, gp)
    rows = steps * bsz
    kern = functools.partial(_s5_kernel, steps=steps)
    blk = pl.BlockSpec((rows, GROUP_W), lambda i: (i, 0))
    return pl.pallas_call(
        kern,
        grid=(n // rows,),
        in_specs=[blk,
                  _full_spec((1, gp)), _full_spec((1, gp)), _full_spec((1, gp)),
                  _full_spec((GROUP_W, 2 * gp)), _full_spec((2 * gp, GROUP_W)),
                  _full_spec((1, GROUP_W)), _full_spec((GROUP_W, GROUP_W)), _full_spec((1, GROUP_W))],
        out_specs=blk,
        out_shape=jax.ShapeDtypeStruct((n, GROUP_W), F32),
        scratch_shapes=[pltpu.VMEM((rows, 2 * gp), F32), pltpu.VMEM((rows, 2 * gp), F32),
                        pltpu.VMEM((bsz, 2 * gp), F32)],
        compiler_params=_params(("arbitrary",)),
        name="s5_mixer",
    )(u_tm, lam_re.reshape(1, gp), lam_im.reshape(1, gp), dt, bmat, cmat,
      d.reshape(1, GROUP_W), w_glu.astype(BF16), b_glu.reshape(1, GROUP_W))


def _hdot(a_c, b):
    return _bdot(a_c, _stack_heads(b))


def _unit_lower_inverse_apply(a, rhs):
    ti, si = _iota2(a.shape, 0), _iota2(a.shape, 1) % HEAD_DIM
    eye = (ti == si).astype(F32)
    a_d = jnp.where((ti // 16) == (si // 16), a, 0.0)
    a_o = a - a_d
    x = eye - a_d
    p = _hdot(a_d, a_d)
    x = x + _hdot(x, p)
    p = _hdot(p, p)
    x = x + _hdot(x, p)
    p = _hdot(p, p)
    d_inv = x + _hdot(x, p)
    nn = _hdot(d_inv, a_o)
    w = eye - nn
    w = w + _hdot(w, _hdot(nn, nn))
    return _hdot(w, _hdot(d_inv, rhs))


def _rwkv_kernel(z_ref, mu_ref, w0_ref, w2_ref, a0_ref, a2_ref, g2_ref, kk_ref, ka_ref, rk_ref,
                 lng_ref, lnb_ref, y_ref, prev_scr, n_scr):
    @pl.when(pl.program_id(1) == 0)
    def _():
        prev_scr[...] = jnp.zeros_like(prev_scr)
        n_scr[...] = jnp.zeros_like(n_scr)

    params = tuple(ref[...] for ref in (mu_ref, w0_ref, w2_ref, a0_ref, a2_ref, g2_ref, kk_ref, ka_ref,
                                        rk_ref, lng_ref, lnb_ref))
    chunk = jax.vmap(lambda z, prev, n0, tri: _rwkv_chunk(z, prev, n0, tri, params))
    y, n_new = chunk(z_ref[...], prev_scr[...], n_scr[...], _batched_tri(z_ref.shape[0]))
    prev_scr[...] = z_ref[:, CHUNK - 8:CHUNK, :]
    n_scr[...] = n_new
    y_ref[...] = y


def _batched_tri(nb):
    shape = (nb, CHUNK, CHUNK)
    return (_iota2(shape, 1) >= _iota2(shape, 2)).astype(F32)


def _rwkv_chunk(z, prev, n0, tri, params):
    mu, w0, w2, a0, a2, g2, k_k, k_a, r_k, ln_g, ln_b = params
    L = CHUNK
    zs = _shift_rows(z, prev, 1)
    zm = z + (zs - z) * mu
    r, k, v = zm[:, 0:256], zm[:, 256:512], zm[:, 512:768]
    slab, xg = zm[:, 768:768 + RW_LORA], zm[:, 896:1024]

    w_raw = -_softplus(-(w0 + _bdot(jnp.tanh(slab), w2))) - 0.5
    lw = -jnp.exp(w_raw)
    a = _sigmoid(a0 + _bdot(slab, a2))
    g = _bdot(_sigmoid(xg), g2)

    hones = _head_ones()
    kk = k * k_k
    kt = k * (1.0 + (a - 1.0) * k_a)
    sums = _bdot(jnp.concatenate([kk * kk, r * kt * r_k], axis=0), hones)
    kkn = kk / jnp.maximum(jnp.sqrt(sums[:L]), 1e-12)
    bonus = sums[L:]
    beta = kkn * a

    cum = _sdot_left(tri, lw)
    cum_last = cum[L - 1:L, :]
    e_neg = jnp.exp(-cum)
    e_tail = jnp.exp(cum_last - cum)
    qr = jnp.concatenate([kkn * jnp.exp(cum - lw), r * jnp.exp(cum)], axis=0)
    kkb_s = jnp.concatenate([_stack_heads(kt * e_neg), _stack_heads(beta * e_neg)], axis=0)
    prod = _bdot_nt(qr, kkb_s)
    ti, si = _iota2((L, GROUP_W), 0), _iota2((L, GROUP_W), 1) % HEAD_DIM
    strict, incl = ti > si, ti >= si
    a_k = jnp.where(strict, prod[:L, :GROUP_W], 0.0)
    a_b = jnp.where(strict, prod[:L, GROUP_W:], 0.0)
    b_k = jnp.where(incl, prod[L:, :GROUP_W], 0.0)
    b_b = jnp.where(incl, prod[L:, GROUP_W:], 0.0)

    reads = _bdot_nt(qr, n0)
    v_s = _stack_heads(v)
    u_w = _unit_lower_inverse_apply(a_b, reads[:L] + _bdot(a_k, v_s))
    y = reads[L:] + _bdot(b_k, v_s) - _hdot(b_b, u_w)

    same_head = (_iota2((GROUP_W, GROUP_W), 0) // HEAD_DIM) == (_iota2((GROUP_W, GROUP_W), 1) // HEAD_DIM)
    upd = _bdot_tn(jnp.concatenate([v, -u_w], axis=0),
                   jnp.concatenate([kt * e_tail, beta * e_tail], axis=0))
    n_new = n0 * jnp.exp(cum_last) + jnp.where(same_head, upd, 0.0)

    inv_hd = 1.0 / HEAD_DIM
    mean = _sdot(y, hones) * inv_hd
    yc = y - mean
    var = _sdot(yc * yc, hones) * inv_hd
    yn = yc * lax.rsqrt(var + RW_GN_EPS) * ln_g + ln_b
    return (yn + bonus * v) * g, n_new


def _rwkv_call(z_rw, mu, w0, w2, a0, a2, g2, k_k, k_a, r_k, ln_g, ln_b, nb=8):
    bsz, seq, width = z_rw.shape
    row = lambda t: t.reshape(1, -1)
    zeros = jnp.zeros((RW_LORA // 2, GROUP_W), F32)
    w2p = jnp.concatenate([w2, zeros], axis=0).astype(BF16)
    a2p = jnp.concatenate([zeros, a2], axis=0).astype(BF16)
    vec = _full_spec((1, GROUP_W))
    mat = _full_spec((RW_LORA, GROUP_W))
    return pl.pallas_call(
        _rwkv_kernel,
        grid=(bsz // nb, seq // CHUNK),
        in_specs=[pl.BlockSpec((nb, CHUNK, width), lambda b, i: (b, i, 0)),
                  _full_spec((1, width)), vec, mat, vec, mat, mat, vec, vec, vec, vec, vec],
        out_specs=pl.BlockSpec((nb, CHUNK, GROUP_W), lambda b, i: (b, i, 0)),
        out_shape=jax.ShapeDtypeStruct((bsz, seq, GROUP_W), F32),
        scratch_shapes=[pltpu.VMEM((nb, 8, width), F32), pltpu.VMEM((nb, GROUP_W, GROUP_W), F32)],
        compiler_params=_params(("arbitrary", "arbitrary")),
        name="rwkv7_mixer",
    )(z_rw, row(mu), row(w0), w2p, row(a0), a2p, g2.astype(BF16), row(k_k), row(k_a), row(r_k),
      row(ln_g), row(ln_b))


def _sb_kernel(q_ref, k_ref, v_ref, o_ref, *, blk):
    qi = pl.program_id(1)
    n4 = N_HEADS * blk
    q_s = _stack_heads(q_ref[...] * (HEAD_DIM ** -0.5)).astype(BF16)
    later = (_iota2((blk, blk), 0) > _iota2((blk, blk), 1)).astype(F32)

    def key_block(kb, carry, acc, diagonal):
        start = pl.multiple_of(kb * blk, blk)
        kblk = k_ref[pl.ds(start, blk), :]
        vblk = v_ref[pl.ds(start, blk), :]
        logits = _bdot_nt(q_s, kblk)
        sp = _softplus(logits)
        if diagonal:
            mask = _iota2((n4, blk), 1) < _iota2((n4, blk), 0) % blk
            sp_m = jnp.where(mask, sp, 0.0)
        else:
            sp_m = sp
        after = carry + _sdot(sp_m, later)
        wgt = jnp.exp(logits - sp - after)
        if diagonal:
            wgt = jnp.where(mask, wgt, 0.0)
        wgt_w = jnp.concatenate([wgt[h * blk:(h + 1) * blk] for h in range(N_HEADS)], axis=1)
        acc = acc + _bdot(wgt_w, _stack_heads(vblk))
        return carry + jnp.sum(sp_m, axis=1, keepdims=True), acc

    def cond(state):
        kb, carry, _ = state
        return jnp.logical_and(kb >= 0, jnp.min(carry) < SB_SKIP_LOG)

    def body(state):
        kb, carry, acc = state
        carry, acc = key_block(kb, carry, acc, diagonal=False)
        return kb - 1, carry, acc

    carry, acc = key_block(qi, jnp.zeros((n4, 1), F32), jnp.zeros((blk, GROUP_W), F32), diagonal=True)
    _, _, acc = lax.while_loop(cond, body, (qi - 1, carry, acc))
    o_ref[...] = acc


def _sb_call(z_sb, blk=256):
    bsz, seq, _ = z_sb.shape
    kern = functools.partial(_sb_kernel, blk=blk)
    return pl.pallas_call(
        kern,
        grid=(bsz, seq // blk),
        in_specs=[pl.BlockSpec((None, blk, GROUP_W), lambda b, i: (b, i, 0)),
                  pl.BlockSpec((None, seq, GROUP_W), lambda b, i: (b, 0, 1)),
                  pl.BlockSpec((None, seq, GROUP_W), lambda b, i: (b, 0, 2))],
        out_specs=pl.BlockSpec((None, blk, GROUP_W), lambda b, i: (b, i, 0)),
        out_shape=jax.ShapeDtypeStruct((bsz, seq, GROUP_W), F32),
        compiler_params=_params(("parallel", "arbitrary")),
        name="stickbreak_attn",
    )(z_sb, z_sb, z_sb)


def _mlstm_kernel(z_ref, g_ref, cw_ref, cb_ref, bi_ref, bf_ref, lng_ref, y_ref,
                  prev_scr, c_scr, n_scr, m_scr):
    @pl.when(pl.program_id(1) == 0)
    def _():
        prev_scr[...] = jnp.zeros_like(prev_scr)
        c_scr[...] = jnp.zeros_like(c_scr)
        n_scr[...] = jnp.zeros_like(n_scr)
        m_scr[...] = jnp.full(m_scr.shape, -jnp.inf, F32)

    params = tuple(ref[...] for ref in (cw_ref, cb_ref, bi_ref, bf_ref, lng_ref))
    chunk = jax.vmap(lambda z, gz, prev, state, tri: _mlstm_chunk(z, gz, prev, state, tri, params))
    state = (c_scr[...], n_scr[:, 0:1, :], m_scr[:, 0:1, :])
    y, (c_new, n_new, m_new) = chunk(z_ref[...], g_ref[...], prev_scr[...], state,
                                     _batched_tri(z_ref.shape[0]))
    prev_scr[...] = z_ref[:, CHUNK - 8:CHUNK, 0:2 * GROUP_W]
    c_scr[...] = c_new
    n_scr[...] = jnp.broadcast_to(n_new, n_scr.shape)
    m_scr[...] = jnp.broadcast_to(m_new, m_scr.shape)
    y_ref[...] = y


def _mlstm_chunk(z, gz, prev, state, tri, params):
    cw, cb, b_i, b_f, ln_g = params
    c_prev, n_prev, m_prev = state
    L = CHUNK
    qk_raw = z[:, 0:2 * GROUP_W]
    conv = qk_raw * cw[3:4] + cb
    for j in range(3):
        conv = conv + _shift_rows(qk_raw, prev, 3 - j) * cw[j:j + 1]
    qk = conv * _sigmoid(conv)
    q, k = qk[:, :GROUP_W], qk[:, GROUP_W:] * (HEAD_DIM ** -0.5)
    v, o = z[:, 2 * GROUP_W:3 * GROUP_W], z[:, 3 * GROUP_W:]

    gate_col = _iota2((gz.shape[1], GROUP_W), 0)
    gate_head = _iota2((gz.shape[1], GROUP_W), 1) // HEAD_DIM
    ig = _sdot(gz, (gate_col == gate_head).astype(F32)) + b_i
    log_f = -_softplus(-(gz + b_f))
    log_f = _sdot(log_f, (gate_col == gate_head + N_HEADS).astype(F32))

    b = _xdot_left(tri, log_f)
    b_last = b[L - 1:L, :]

    gl = b_last - b + ig
    m_loc = jnp.max(gl, axis=0, keepdims=True)
    kw = k * jnp.exp(gl - m_loc)
    same_head = (_iota2((GROUP_W, GROUP_W), 0) // HEAD_DIM) == (_iota2((GROUP_W, GROUP_W), 1) // HEAD_DIM)
    c_loc = jnp.where(same_head, _bdot_tn(kw, v), 0.0)
    n_loc = jnp.sum(kw, axis=0, keepdims=True)
    m_new = jnp.maximum(b_last + m_prev, m_loc)
    s_old, s_new = jnp.exp(b_last + m_prev - m_new), jnp.exp(m_loc - m_new)
    new_state = (s_old * c_prev + s_new * c_loc, s_old * n_prev + s_new * n_loc, m_new)

    lane = _iota2((L, GROUP_W), 1)
    lane_head, lane_s = lane // HEAD_DIM, lane % HEAD_DIM
    row_l = _iota2((L, GROUP_W), 0)
    x = ig - b
    x_row = _xdot_left(jnp.maximum(tri, 1.0), jnp.where(row_l == lane_s, x, 0.0))
    log_d = b + x_row
    causal = lane_s <= row_l
    m_intra = jnp.full((L, GROUP_W), -jnp.inf, F32)
    for h in range(N_HEADS):
        sel = lane_head == h
        mh = jnp.max(jnp.where(jnp.logical_and(sel, causal), log_d, -jnp.inf), axis=1, keepdims=True)
        m_intra = jnp.where(sel, mh, m_intra)
    log_inter = b + m_prev
    m_t = jnp.maximum(log_inter, m_intra)
    w_intra = jnp.where(causal, jnp.exp(log_d - m_t), 0.0)
    w_inter = jnp.exp(log_inter - m_t)

    hones = _head_ones()
    s_mat = _bdot_nt(q, _stack_heads(k)) * w_intra
    num = w_inter * _bdot(q, c_prev) + _bdot(s_mat, _stack_heads(v))
    den = w_inter * _bdot(q * n_prev, hones) + _bdot(s_mat, hones)
    hval = num / jnp.maximum(jnp.abs(den), jnp.exp(-m_t))

    inv_hd = 1.0 / HEAD_DIM
    mean = _sdot(hval, hones) * inv_hd
    hc = hval - mean
    var = _sdot(hc * hc, hones) * inv_hd
    return _sigmoid(o) * (hc * lax.rsqrt(var + LN_EPS) * ln_g), new_state


def _mlstm_call(z_ml, z_gate, conv_w, conv_b, b_i, b_f, ln_g, nb=8):
    bsz, seq, width = z_ml.shape
    gw = z_gate.shape[2]
    lanes = lambda t: jnp.repeat(t, HEAD_DIM).reshape(1, GROUP_W)
    b_f_cols = jnp.pad(b_f, (N_HEADS, gw - 2 * N_HEADS)).reshape(1, gw)
    vec = _full_spec((1, GROUP_W))
    return pl.pallas_call(
        _mlstm_kernel,
        grid=(bsz // nb, seq // CHUNK),
        in_specs=[pl.BlockSpec((nb, CHUNK, width), lambda b, i: (b, i, 0)),
                  pl.BlockSpec((nb, CHUNK, gw), lambda b, i: (b, i, 0)),
                  _full_spec(conv_w.shape), _full_spec((1, 2 * GROUP_W)), vec, _full_spec((1, gw)), vec],
        out_specs=pl.BlockSpec((nb, CHUNK, GROUP_W), lambda b, i: (b, i, 0)),
        out_shape=jax.ShapeDtypeStruct((bsz, seq, GROUP_W), F32),
        scratch_shapes=[pltpu.VMEM((nb, 8, 2 * GROUP_W), F32), pltpu.VMEM((nb, GROUP_W, GROUP_W), F32),
                        pltpu.VMEM((nb, 8, GROUP_W), F32), pltpu.VMEM((nb, 8, GROUP_W), F32)],
        compiler_params=_params(("arbitrary", "arbitrary")),
        name="mlstm_mixer",
    )(z_ml, z_gate, conv_w, conv_b.reshape(1, -1), lanes(b_i), b_f_cols, ln_g.reshape(1, -1))


def _resident_spec(shape):
    nd = len(shape)
    return pl.BlockSpec(shape, lambda *_: (0,) * nd, pipeline_mode=pl.Buffered(1))


def _mix_ffn_kernel(h_ref, y0_ref, y1_ref, y2_ref, y3_ref, wout_ref, g1_ref, b1_ref,
                    wup_ref, cw_ref, cb_ref, wdn_ref, g_ref, b_ref, o_ref, prev_scr, *, alpha, d_ff):
    @pl.when(pl.program_id(1) == 0)
    def _():
        prev_scr[...] = jnp.zeros_like(prev_scr)

    acc = alpha * h_ref[...]
    for i, y_ref in enumerate((y0_ref, y1_ref, y2_ref, y3_ref)):
        acc = acc + jnp.dot(y_ref[...].astype(BF16), wout_ref[i * GROUP_W:(i + 1) * GROUP_W, :],
                            preferred_element_type=F32)
    h = _layer_norm(acc, g1_ref[...], b1_ref[...])

    rows = h.shape[0]
    up = jnp.dot(h.astype(BF16), wup_ref[...], preferred_element_type=F32)
    u, gate = up[:, :d_ff], up[:, d_ff:]
    prev = prev_scr[...]
    prev_scr[...] = u[rows - 8:rows]
    cw = cw_ref[...]
    conv = u * cw[2:3] + cb_ref[...]
    conv = conv + _shift_rows(u, prev, 1) * cw[1:2] + _shift_rows(u, prev, 2) * cw[0:1]
    act = (_gelu_tanh(conv) * gate).astype(BF16)
    down = jnp.dot(act, wdn_ref[...], preferred_element_type=F32)
    o_ref[...] = _layer_norm(alpha * h + down, g_ref[...], b_ref[...])


def _mix_ffn_call(h, ys, w_out, g1, b1, w_up, conv_w, conv_b, w_down, g2, b2, alpha, rows=512):
    bsz, seq, d = h.shape
    d_ff = w_down.shape[0]
    blk = lambda w: pl.BlockSpec((None, rows, w), lambda bi, i: (bi, i, 0))
    vec = _full_spec((1, d))
    return pl.pallas_call(
        functools.partial(_mix_ffn_kernel, alpha=alpha, d_ff=d_ff),
        grid=(bsz, seq // rows),
        in_specs=[blk(d), pl.BlockSpec((rows, GROUP_W), lambda bi, i: (i, bi))] + [blk(GROUP_W)] * 3
        + [_resident_spec(w_out.shape), vec, vec,
           _resident_spec(w_up.shape), _full_spec(conv_w.shape), _full_spec((1, d_ff)),
           _resident_spec(w_down.shape), vec, vec],
        out_specs=blk(d),
        out_shape=jax.ShapeDtypeStruct((bsz, seq, d), F32),
        scratch_shapes=[pltpu.VMEM((8, d_ff), F32)],
        compiler_params=_params(("arbitrary", "arbitrary")),
        name="mix_ffn_ln",
    )(h, *ys, w_out.astype(BF16), g1.reshape(1, d), b1.reshape(1, d),
      w_up.astype(BF16), conv_w, conv_b.reshape(1, d_ff), w_down.astype(BF16),
      g2.reshape(1, d), b2.reshape(1, d))


DEPTH = 2
DN_ALPHA = (2 * DEPTH) ** 0.25
IN_SIZES = (256, 1024, 768, 1024, 8)
GATE_PAD = 128
S5_STEPS = 128
ROW_TILE = 512


def kernel(x, ln_in_g, ln_in_b, w_in, s5_lambda_re, s5_lambda_im, s5_log_dt, s5_b_re, s5_b_im, s5_c_re, s5_c_im, s5_d, s5_w_glu, s5_b_glu, rw_mu, rw_w0, rw_w2, rw_a0, rw_a2, rw_g2, rw_k_k, rw_k_a, rw_r_k, rw_ln_g, rw_ln_b, ml_conv_w, ml_conv_b, ml_b_i, ml_b_f, ml_ln_g, w_out, ln1_g, ln1_b, ffn_w_up, ffn_conv_w, ffn_conv_b, ffn_w_down, ln2_g, ln2_b):
    bsz, seq, d = x.shape
    n = bsz * seq
    h = x.reshape(n, d)
    cuts = [0]
    for s in IN_SIZES:
        cuts.append(cuts[-1] + s)
    for l in range(DEPTH):
        ws = [w_in[l][:, cuts[i]:cuts[i + 1]].astype(BF16) for i in range(len(IN_SIZES))]
        ws[4] = jnp.pad(ws[4], ((0, 0), (0, GATE_PAD - IN_SIZES[4])))
        if l == 0:
            h, z_s5, z_rw, z_sb, z_ml, z_g = _inproj_call(h, ws, ROW_TILE, bsz, ln=(ln_in_g, ln_in_b))
        else:
            z_s5, z_rw, z_sb, z_ml, z_g = _inproj_call(h, ws, ROW_TILE, bsz)

        y_s5 = _s5_call(z_s5.reshape(n, GROUP_W), bsz, s5_lambda_re[l], s5_lambda_im[l], s5_log_dt[l],
                        s5_b_re[l], s5_b_im[l], s5_c_re[l], s5_c_im[l], s5_d[l], s5_w_glu[l], s5_b_glu[l],
                        S5_STEPS).reshape(seq, bsz * GROUP_W)
        y_rw = _rwkv_call(z_rw.reshape(bsz, seq, -1), rw_mu[l], rw_w0[l], rw_w2[l], rw_a0[l], rw_a2[l],
                          rw_g2[l], rw_k_k[l], rw_k_a[l], rw_r_k[l], rw_ln_g[l], rw_ln_b[l])
        y_sb = _sb_call(z_sb.reshape(bsz, seq, -1))
        y_ml = _mlstm_call(z_ml.reshape(bsz, seq, -1), z_g.reshape(bsz, seq, -1), ml_conv_w[l],
                           ml_conv_b[l], ml_b_i[l], ml_b_f[l], ml_ln_g[l])
        h = _mix_ffn_call(h.reshape(bsz, seq, d), [y_s5, y_rw, y_sb, y_ml], w_out[l], ln1_g[l], ln1_b[l],
                          ffn_w_up[l], ffn_conv_w[l], ffn_conv_b[l], ffn_w_down[l], ln2_g[l], ln2_b[l],
                          DN_ALPHA).reshape(n, d)
    return h.reshape(bsz, seq, d)
```

```python
import functools
import math

import jax
import jax.numpy as jnp
from jax import lax
from jax.experimental import pallas as pl
from jax.experimental.pallas import tpu as pltpu

F32 = jnp.float32
BF16 = jnp.bfloat16

GROUP_W = 256
HEAD_DIM = 64
N_HEADS = GROUP_W // HEAD_DIM
CHUNK = 64
S5_G, S5_P, S5_CH = 16, 64, 16
S5_STATE = S5_G * S5_P
RW_LORA = 128
LN_EPS = 1e-5
RW_GN_EPS = 64e-5
SB_SKIP_LOG = 110.0
VMEM_LIMIT = 56 * 1024 * 1024


def _bdot(a, b):
    return jnp.dot(a.astype(BF16), b.astype(BF16), preferred_element_type=F32)


def _bdot_nt(a, b):
    return lax.dot_general(a.astype(BF16), b.astype(BF16), (((1,), (1,)), ((), ())),
                           preferred_element_type=F32)


def _bdot_tn(a, b):
    return lax.dot_general(a.astype(BF16), b.astype(BF16), (((0,), (0,)), ((), ())),
                           preferred_element_type=F32)


def _split3(x):
    hi = x.astype(BF16)
    r1 = x - hi.astype(F32)
    mid = r1.astype(BF16)
    lo = (r1 - mid.astype(F32)).astype(BF16)
    return hi, mid, lo


def _xdot_left(sel, a):
    hi, mid, lo = _split3(a)
    s = sel.astype(BF16)
    out = jnp.dot(s, lo, preferred_element_type=F32)
    out = out + jnp.dot(s, mid, preferred_element_type=F32)
    return out + jnp.dot(s, hi, preferred_element_type=F32)


def _split2(x):
    hi = x.astype(BF16)
    return hi, (x - hi.astype(F32)).astype(BF16)


def _sdot(a, sel):
    hi, lo = _split2(a)
    s = sel.astype(BF16)
    return jnp.dot(lo, s, preferred_element_type=F32) + jnp.dot(hi, s, preferred_element_type=F32)


def _sdot_left(sel, a):
    hi, lo = _split2(a)
    s = sel.astype(BF16)
    return jnp.dot(s, lo, preferred_element_type=F32) + jnp.dot(s, hi, preferred_element_type=F32)


def _sigmoid(x):
    return 1.0 / (1.0 + jnp.exp(-x))


def _softplus(x):
    return jnp.maximum(x, 0.0) + jnp.log(1.0 + jnp.exp(-jnp.abs(x)))


def _gelu_tanh(x):
    c = math.sqrt(2.0 / math.pi)
    return 0.5 * x * (1.0 + jnp.tanh(c * (x + 0.044715 * (x * x * x))))


def _layer_norm(x, g, b):
    mu = jnp.mean(x, axis=-1, keepdims=True)
    xc = x - mu
    var = jnp.mean(xc * xc, axis=-1, keepdims=True)
    return xc * lax.rsqrt(var + LN_EPS) * g + b


def _iota2(shape, dim):
    return lax.broadcasted_iota(jnp.int32, shape, dim)


def _head_ones():
    r = _iota2((GROUP_W, GROUP_W), 0) // HEAD_DIM
    c = _iota2((GROUP_W, GROUP_W), 1) // HEAD_DIM
    return (r == c).astype(F32)


def _stack_heads(x):
    lane_head = _iota2(x.shape, 1) // HEAD_DIM
    return jnp.concatenate([jnp.where(lane_head == h, x, 0.0) for h in range(N_HEADS)], axis=0)


def _shift_rows(x, prev, n):
    ext = jnp.concatenate([prev, x], axis=0)
    return ext[8 - n:8 - n + x.shape[0]]


def _full_spec(shape):
    nd = len(shape)
    return pl.BlockSpec(shape, lambda *_: (0,) * nd)


def _params(sem):
    return pltpu.CompilerParams(dimension_semantics=sem, vmem_limit_bytes=VMEM_LIMIT)


def _inproj_kernel(h_ref, *refs):
    nw = len(refs) // 2
    hb = h_ref[...].astype(BF16)
    for w_ref, o_ref in zip(refs[:nw], refs[nw:]):
        o_ref[...] = jnp.dot(hb, w_ref[...], preferred_element_type=F32)


def _ln_inproj_kernel(x_ref, g_ref, b_ref, *refs):
    h = _layer_norm(x_ref[...], g_ref[...], b_ref[...])
    refs[len(refs) // 2][...] = h
    hb = h.astype(BF16)
    nw = len(refs) // 2
    for w_ref, o_ref in zip(refs[:nw], refs[nw + 1:]):
        o_ref[...] = jnp.dot(hb, w_ref[...], preferred_element_type=F32)


def _time_major_spec(rows, seq):
    tiles = seq // rows
    return pl.BlockSpec((rows, GROUP_W), lambda i: (i % tiles, i // tiles))


def _inproj_call(h2d, weights, rows, bsz, ln=None):
    n, d = h2d.shape
    seq = n // bsz
    row_spec = lambda w: pl.BlockSpec((rows, w), lambda i: (i, 0))
    out_specs = [row_spec(w.shape[1]) for w in weights]
    out_shape = [jax.ShapeDtypeStruct((n, w.shape[1]), F32) for w in weights]
    out_specs[0] = _time_major_spec(rows, seq)
    out_shape[0] = jax.ShapeDtypeStruct((seq, bsz * GROUP_W), F32)
    in_specs = [row_spec(d)]
    args = [h2d]
    if ln is not None:
        in_specs += [_full_spec((1, d)), _full_spec((1, d))]
        args += [ln[0].reshape(1, d), ln[1].reshape(1, d)]
        out_specs = [row_spec(d)] + out_specs
        out_shape = [jax.ShapeDtypeStruct((n, d), F32)] + out_shape
    return pl.pallas_call(
        _inproj_kernel if ln is None else _ln_inproj_kernel,
        grid=(n // rows,),
        in_specs=in_specs + [_full_spec(w.shape) for w in weights],
        out_specs=out_specs,
        out_shape=out_shape,
        compiler_params=_params(("parallel",)),
        name="in_proj" if ln is None else "ln_in_proj",
    )(*args, *weights)


def _s5_kernel(u_ref, lre_ref, lim_ref, dt_ref, bmat_ref, cmat_ref, d_ref, wglu_ref, bglu_ref,
               y_ref, bu_scr, s_scr, st_scr, *, steps):
    @pl.when(pl.program_id(0) == 0)
    def _():
        st_scr[...] = jnp.zeros_like(st_scr)

    lr, li, dt = lre_ref[...], lim_ref[...], jnp.exp(dt_ref[...])
    mag = jnp.exp(lr * dt)
    ab_re, ab_im = mag * jnp.cos(li * dt), mag * jnp.sin(li * dt)
    den = lr * lr + li * li
    z_re = ((ab_re - 1.0) * lr + ab_im * li) / den
    z_im = (ab_im * lr - (ab_re - 1.0) * li) / den

    bsz = st_scr.shape[0]
    u = u_ref[...]
    p = _bdot(u, bmat_ref[...])
    p_re, p_im = p[:, :S5_STATE], p[:, S5_STATE:]
    bu_scr[:, :S5_STATE] = z_re * p_re - z_im * p_im
    bu_scr[:, S5_STATE:] = z_re * p_im + z_im * p_re

    a_re = jnp.broadcast_to(ab_re, (bsz, S5_STATE))
    a_im = jnp.broadcast_to(ab_im, (bsz, S5_STATE))

    def step(t, carry):
        s_re, s_im = carry
        row = pl.multiple_of(t * bsz, bsz)
        n_re = a_re * s_re - a_im * s_im + bu_scr[pl.ds(row, bsz), :S5_STATE]
        n_im = a_re * s_im + a_im * s_re + bu_scr[pl.ds(row, bsz), S5_STATE:]
        s_scr[pl.ds(row, bsz), :S5_STATE] = n_re
        s_scr[pl.ds(row, bsz), S5_STATE:] = n_im
        return n_re, n_im

    s_re, s_im = lax.fori_loop(0, steps, step, (st_scr[:, :S5_STATE], st_scr[:, S5_STATE:]), unroll=2)
    st_scr[:, :S5_STATE] = s_re
    st_scr[:, S5_STATE:] = s_im

    y = _bdot(s_scr[...], cmat_ref[...]) + d_ref[...] * u
    y = _gelu_tanh(y)
    y_ref[...] = y * _sigmoid(_bdot(y, wglu_ref[...]) + bglu_ref[...])


def _s5_call(u_tm, bsz, lam_re, lam_im, log_dt, b_re, b_im, c_re, c_im, d, w_glu, b_glu, steps):
    n = u_tm.shape[0]
    gp = S5_STATE
    eye_g = jnp.eye(S5_G, dtype=F32)
    b_bd = lambda b: jnp.einsum("gpc,gh->gchp", b, eye_g).reshape(GROUP_W, gp)
    c_bd = lambda c: jnp.einsum("gcp,gh->gphc", c, eye_g).reshape(gp, GROUP_W)
    bmat = jnp.concatenate([b_bd(b_re), b_bd(b_im)], axis=1).astype(BF16)
    cmat = jnp.concatenate([c_bd(c_re), -c_bd(c_im)], axis=0).astype(BF16)
    dt = jnp.repeat(log_dt, S5_P).reshape(1, gp)
    rows = steps * bsz
    kern = functools.partial(_s5_kernel, steps=steps)
    blk = pl.BlockSpec((rows, GROUP_W), lambda i: (i, 0))
    return pl.pallas_call(
        kern,
        grid=(n // rows,),
        in_specs=[blk,
                  _full_spec((1, gp)), _full_spec((1, gp)), _full_spec((1, gp)),
                  _full_spec((GROUP_W, 2 * gp)), _full_spec((2 * gp, GROUP_W)),
                  _full_spec((1, GROUP_W)), _full_spec((GROUP_W, GROUP_W)), _full_spec((1, GROUP_W))],
        out_specs=blk,
        out_shape=jax.ShapeDtypeStruct((n, GROUP_W), F32),
        scratch_shapes=[pltpu.VMEM((rows, 2 * gp), F32), pltpu.VMEM((rows, 2 * gp), F32),
                        pltpu.VMEM((bsz, 2 * gp), F32)],
        compiler_params=_params(("arbitrary",)),
        name="s5_mixer",
    )(u_tm, lam_re.reshape(1, gp), lam_im.reshape(1, gp), dt, bmat, cmat,
      d.reshape(1, GROUP_W), w_glu.astype(BF16), b_glu.reshape(1, GROUP_W))


def _hdot(a_c, b):
    return _bdot(a_c, _stack_heads(b))


def _unit_lower_inverse_apply(a, rhs):
    ti, si = _iota2(a.shape, 0), _iota2(a.shape, 1) % HEAD_DIM
    eye = (ti == si).astype(F32)
    a_d = jnp.where((ti // 16) == (si // 16), a, 0.0)
    a_o = a - a_d
    x = eye - a_d
    p = _hdot(a_d, a_d)
    x = x + _hdot(x, p)
    p = _hdot(p, p)
    x = x + _hdot(x, p)
    p = _hdot(p, p)
    d_inv = x + _hdot(x, p)
    nn = _hdot(d_inv, a_o)
    w = eye - nn
    w = w + _hdot(w, _hdot(nn, nn))
    return _hdot(w, _hdot(d_inv, rhs))


def _rwkv_kernel(z_ref, mu_ref, w0_ref, w2_ref, a0_ref, a2_ref, g2_ref, kk_ref, ka_ref, rk_ref,
                 lng_ref, lnb_ref, y_ref, prev_scr, n_scr):
    @pl.when(pl.program_id(1) == 0)
    def _():
        prev_scr[...] = jnp.zeros_like(prev_scr)
        n_scr[...] = jnp.zeros_like(n_scr)

    params = tuple(ref[...] for ref in (mu_ref, w0_ref, w2_ref, a0_ref, a2_ref, g2_ref, kk_ref, ka_ref,
                                        rk_ref, lng_ref, lnb_ref))
    chunk = jax.vmap(lambda z, prev, n0, tri: _rwkv_chunk(z, prev, n0, tri, params))
    tri = _batched_tri(z_ref.shape[0])
    prev, state = prev_scr[...], n_scr[...]
    for c in range(z_ref.shape[1] // CHUNK):
        rows = slice(c * CHUNK, (c + 1) * CHUNK)
        y_ref[:, rows, :], state = chunk(z_ref[:, rows, :], prev, state, tri)
        prev = z_ref[:, (c + 1) * CHUNK - 8:(c + 1) * CHUNK, :]
    prev_scr[...] = prev
    n_scr[...] = state


def _batched_tri(nb):
    shape = (nb, CHUNK, CHUNK)
    return (_iota2(shape, 1) >= _iota2(shape, 2)).astype(F32)


def _rwkv_chunk(z, prev, n0, tri, params):
    mu, w0, w2, a0, a2, g2, k_k, k_a, r_k, ln_g, ln_b = params
    L = CHUNK
    zs = _shift_rows(z, prev, 1)
    zm = z + (zs - z) * mu
    r, k, v = zm[:, 0:256], zm[:, 256:512], zm[:, 512:768]
    slab, xg = zm[:, 768:768 + RW_LORA], zm[:, 896:1024]

    w_raw = -_softplus(-(w0 + _bdot(jnp.tanh(slab), w2))) - 0.5
    lw = -jnp.exp(w_raw)
    a = _sigmoid(a0 + _bdot(slab, a2))
    g = _bdot(_sigmoid(xg), g2)

    hones = _head_ones()
    kk = k * k_k
    kt = k * (1.0 + (a - 1.0) * k_a)
    sums = _bdot(jnp.concatenate([kk * kk, r * kt * r_k], axis=0), hones)
    kkn = kk / jnp.maximum(jnp.sqrt(sums[:L]), 1e-12)
    bonus = sums[L:]
    beta = kkn * a

    cum = _sdot_left(tri, lw)
    cum_last = cum[L - 1:L, :]
    e_neg = jnp.exp(-cum)
    e_tail = jnp.exp(cum_last - cum)
    qr = jnp.concatenate([kkn * jnp.exp(cum - lw), r * jnp.exp(cum)], axis=0)
    kkb_s = jnp.concatenate([_stack_heads(kt * e_neg), _stack_heads(beta * e_neg)], axis=0)
    prod = _bdot_nt(qr, kkb_s)
    ti, si = _iota2((L, GROUP_W), 0), _iota2((L, GROUP_W), 1) % HEAD_DIM
    strict, incl = ti > si, ti >= si
    a_k = jnp.where(strict, prod[:L, :GROUP_W], 0.0)
    a_b = jnp.where(strict, prod[:L, GROUP_W:], 0.0)
    b_k = jnp.where(incl, prod[L:, :GROUP_W], 0.0)
    b_b = jnp.where(incl, prod[L:, GROUP_W:], 0.0)

    reads = _bdot_nt(qr, n0)
    v_s = _stack_heads(v)
    u_w = _unit_lower_inverse_apply(a_b, reads[:L] + _bdot(a_k, v_s))
    y = reads[L:] + _bdot(b_k, v_s) - _hdot(b_b, u_w)

    same_head = (_iota2((GROUP_W, GROUP_W), 0) // HEAD_DIM) == (_iota2((GROUP_W, GROUP_W), 1) // HEAD_DIM)
    upd = _bdot_tn(jnp.concatenate([v, -u_w], axis=0),
                   jnp.concatenate([kt * e_tail, beta * e_tail], axis=0))
    n_new = n0 * jnp.exp(cum_last) + jnp.where(same_head, upd, 0.0)

    inv_hd = 1.0 / HEAD_DIM
    mean = _sdot(y, hones) * inv_hd
    yc = y - mean
    var = _sdot(yc * yc, hones) * inv_hd
    yn = yc * lax.rsqrt(var + RW_GN_EPS) * ln_g + ln_b
    return (yn + bonus * v) * g, n_new


def _rwkv_call(z_rw, mu, w0, w2, a0, a2, g2, k_k, k_a, r_k, ln_g, ln_b, nb=8, chunks=4):
    bsz, seq, width = z_rw.shape
    rows = chunks * CHUNK
    row = lambda t: t.reshape(1, -1)
    zeros = jnp.zeros((RW_LORA // 2, GROUP_W), F32)
    w2p = jnp.concatenate([w2, zeros], axis=0).astype(BF16)
    a2p = jnp.concatenate([zeros, a2], axis=0).astype(BF16)
    vec = _full_spec((1, GROUP_W))
    mat = _full_spec((RW_LORA, GROUP_W))
    return pl.pallas_call(
        _rwkv_kernel,
        grid=(bsz // nb, seq // rows),
        in_specs=[pl.BlockSpec((nb, rows, width), lambda b, i: (b, i, 0)),
                  _full_spec((1, width)), vec, mat, vec, mat, mat, vec, vec, vec, vec, vec],
        out_specs=pl.BlockSpec((nb, rows, GROUP_W), lambda b, i: (b, i, 0)),
        out_shape=jax.ShapeDtypeStruct((bsz, seq, GROUP_W), F32),
        scratch_shapes=[pltpu.VMEM((nb, 8, width), F32), pltpu.VMEM((nb, GROUP_W, GROUP_W), F32)],
        compiler_params=_params(("arbitrary", "arbitrary")),
        name="rwkv7_mixer",
    )(z_rw, row(mu), row(w0), w2p, row(a0), a2p, g2.astype(BF16), row(k_k), row(k_a), row(r_k),
      row(ln_g), row(ln_b))


def _sb_kernel(q_ref, k_ref, v_ref, o_ref, *, blk):
    qi = pl.program_id(1)
    n4 = N_HEADS * blk
    q_s = _stack_heads(q_ref[...] * (HEAD_DIM ** -0.5)).astype(BF16)
    later = (_iota2((blk, blk), 0) > _iota2((blk, blk), 1)).astype(F32)

    def key_block(kb, carry, acc, diagonal):
        start = pl.multiple_of(kb * blk, blk)
        kblk = k_ref[pl.ds(start, blk), :]
        vblk = v_ref[pl.ds(start, blk), :]
        logits = _bdot_nt(q_s, kblk)
        sp = _softplus(logits)
        if diagonal:
            mask = _iota2((n4, blk), 1) < _iota2((n4, blk), 0) % blk
            sp_m = jnp.where(mask, sp, 0.0)
        else:
            sp_m = sp
        after = carry + _sdot(sp_m, later)
        wgt = jnp.exp(logits - sp - after)
        if diagonal:
            wgt = jnp.where(mask, wgt, 0.0)
        wgt_w = jnp.concatenate([wgt[h * blk:(h + 1) * blk] for h in range(N_HEADS)], axis=1)
        acc = acc + _bdot(wgt_w, _stack_heads(vblk))
        return carry + jnp.sum(sp_m, axis=1, keepdims=True), acc

    def cond(state):
        kb, carry, _ = state
        return jnp.logical_and(kb >= 0, jnp.min(carry) < SB_SKIP_LOG)

    def body(state):
        kb, carry, acc = state
        carry, acc = key_block(kb, carry, acc, diagonal=False)
        return kb - 1, carry, acc

    carry, acc = key_block(qi, jnp.zeros((n4, 1), F32), jnp.zeros((blk, GROUP_W), F32), diagonal=True)
    _, _, acc = lax.while_loop(cond, body, (qi - 1, carry, acc))
    o_ref[...] = acc


def _sb_call(z_sb, blk=256):
    bsz, seq, _ = z_sb.shape
    kern = functools.partial(_sb_kernel, blk=blk)
    return pl.pallas_call(
        kern,
        grid=(bsz, seq // blk),
        in_specs=[pl.BlockSpec((None, blk, GROUP_W), lambda b, i: (b, i, 0)),
                  pl.BlockSpec((None, seq, GROUP_W), lambda b, i: (b, 0, 1)),
                  pl.BlockSpec((None, seq, GROUP_W), lambda b, i: (b, 0, 2))],
        out_specs=pl.BlockSpec((None, blk, GROUP_W), lambda b, i: (b, i, 0)),
        out_shape=jax.ShapeDtypeStruct((bsz, seq, GROUP_W), F32),
        compiler_params=_params(("parallel", "arbitrary")),
        name="stickbreak_attn",
    )(z_sb, z_sb, z_sb)


def _mlstm_kernel(z_ref, g_ref, cw_ref, cb_ref, bi_ref, bf_ref, lng_ref, y_ref,
                  prev_scr, c_scr, n_scr, m_scr):
    @pl.when(pl.program_id(1) == 0)
    def _():
        prev_scr[...] = jnp.zeros_like(prev_scr)
        c_scr[...] = jnp.zeros_like(c_scr)
        n_scr[...] = jnp.zeros_like(n_scr)
        m_scr[...] = jnp.full(m_scr.shape, -jnp.inf, F32)

    params = tuple(ref[...] for ref in (cw_ref, cb_ref, bi_ref, bf_ref, lng_ref))
    chunk = jax.vmap(lambda z, gz, prev, state, tri: _mlstm_chunk(z, gz, prev, state, tri, params))
    tri = _batched_tri(z_ref.shape[0])
    prev, state = prev_scr[...], (c_scr[...], n_scr[:, 0:1, :], m_scr[:, 0:1, :])
    for c in range(z_ref.shape[1] // CHUNK):
        rows = slice(c * CHUNK, (c + 1) * CHUNK)
        y_ref[:, rows, :], state = chunk(z_ref[:, rows, :], g_ref[:, rows, :], prev, state, tri)
        prev = z_ref[:, (c + 1) * CHUNK - 8:(c + 1) * CHUNK, 0:2 * GROUP_W]
    prev_scr[...] = prev
    c_scr[...] = state[0]
    n_scr[...] = jnp.broadcast_to(state[1], n_scr.shape)
    m_scr[...] = jnp.broadcast_to(state[2], m_scr.shape)


def _mlstm_chunk(z, gz, prev, state, tri, params):
    cw, cb, b_i, b_f, ln_g = params
    c_prev, n_prev, m_prev = state
    L = CHUNK
    qk_raw = z[:, 0:2 * GROUP_W]
    conv = qk_raw * cw[3:4] + cb
    for j in range(3):
        conv = conv + _shift_rows(qk_raw, prev, 3 - j) * cw[j:j + 1]
    qk = conv * _sigmoid(conv)
    q, k = qk[:, :GROUP_W], qk[:, GROUP_W:] * (HEAD_DIM ** -0.5)
    v, o = z[:, 2 * GROUP_W:3 * GROUP_W], z[:, 3 * GROUP_W:]

    gate_col = _iota2((gz.shape[1], GROUP_W), 0)
    gate_head = _iota2((gz.shape[1], GROUP_W), 1) // HEAD_DIM
    ig = _sdot(gz, (gate_col == gate_head).astype(F32)) + b_i
    log_f = -_softplus(-(gz + b_f))
    log_f = _sdot(log_f, (gate_col == gate_head + N_HEADS).astype(F32))

    b = _xdot_left(tri, log_f)
    b_last = b[L - 1:L, :]

    gl = b_last - b + ig
    m_loc = jnp.max(gl, axis=0, keepdims=True)
    kw = k * jnp.exp(gl - m_loc)
    same_head = (_iota2((GROUP_W, GROUP_W), 0) // HEAD_DIM) == (_iota2((GROUP_W, GROUP_W), 1) // HEAD_DIM)
    c_loc = jnp.where(same_head, _bdot_tn(kw, v), 0.0)
    n_loc = jnp.sum(kw, axis=0, keepdims=True)
    m_new = jnp.maximum(b_last + m_prev, m_loc)
    s_old, s_new = jnp.exp(b_last + m_prev - m_new), jnp.exp(m_loc - m_new)
    new_state = (s_old * c_prev + s_new * c_loc, s_old * n_prev + s_new * n_loc, m_new)

    lane = _iota2((L, GROUP_W), 1)
    lane_head, lane_s = lane // HEAD_DIM, lane % HEAD_DIM
    row_l = _iota2((L, GROUP_W), 0)
    x = ig - b
    x_row = _xdot_left(jnp.maximum(tri, 1.0), jnp.where(row_l == lane_s, x, 0.0))
    log_d = b + x_row
    causal = lane_s <= row_l
    m_intra = jnp.full((L, GROUP_W), -jnp.inf, F32)
    for h in range(N_HEADS):
        sel = lane_head == h
        mh = jnp.max(jnp.where(jnp.logical_and(sel, causal), log_d, -jnp.inf), axis=1, keepdims=True)
        m_intra = jnp.where(sel, mh, m_intra)
    log_inter = b + m_prev
    m_t = jnp.maximum(log_inter, m_intra)
    w_intra = jnp.where(causal, jnp.exp(log_d - m_t), 0.0)
    w_inter = jnp.exp(log_inter - m_t)

    hones = _head_ones()
    s_mat = _bdot_nt(q, _stack_heads(k)) * w_intra
    num = w_inter * _bdot(q, c_prev) + _bdot(s_mat, _stack_heads(v))
    den = w_inter * _bdot(q * n_prev, hones) + _bdot(s_mat, hones)
    hval = num / jnp.maximum(jnp.abs(den), jnp.exp(-m_t))

    inv_hd = 1.0 / HEAD_DIM
    mean = _sdot(hval, hones) * inv_hd
    hc = hval - mean
    var = _sdot(hc * hc, hones) * inv_hd
    return _sigmoid(o) * (hc * lax.rsqrt(var + LN_EPS) * ln_g), new_state


def _mlstm_call(z_ml, z_gate, conv_w, conv_b, b_i, b_f, ln_g, nb=8, chunks=4):
    bsz, seq, width = z_ml.shape
    gw = z_gate.shape[2]
    rows = chunks * CHUNK
    lanes = lambda t: jnp.repeat(t, HEAD_DIM).reshape(1, GROUP_W)
    b_f_cols = jnp.pad(b_f, (N_HEADS, gw - 2 * N_HEADS)).reshape(1, gw)
    vec = _full_spec((1, GROUP_W))
    return pl.pallas_call(
        _mlstm_kernel,
        grid=(bsz // nb, seq // rows),
        in_specs=[pl.BlockSpec((nb, rows, width), lambda b, i: (b, i, 0)),
                  pl.BlockSpec((nb, rows, gw), lambda b, i: (b, i, 0)),
                  _full_spec(conv_w.shape), _full_spec((1, 2 * GROUP_W)), vec, _full_spec((1, gw)), vec],
        out_specs=pl.BlockSpec((nb, rows, GROUP_W), lambda b, i: (b, i, 0)),
        out_shape=jax.ShapeDtypeStruct((bsz, seq, GROUP_W), F32),
        scratch_shapes=[pltpu.VMEM((nb, 8, 2 * GROUP_W), F32), pltpu.VMEM((nb, GROUP_W, GROUP_W), F32),
                        pltpu.VMEM((nb, 8, GROUP_W), F32), pltpu.VMEM((nb, 8, GROUP_W), F32)],
        compiler_params=_params(("arbitrary", "arbitrary")),
        name="mlstm_mixer",
    )(z_ml, z_gate, conv_w, conv_b.reshape(1, -1), lanes(b_i), b_f_cols, ln_g.reshape(1, -1))


def _resident_spec(shape):
    nd = len(shape)
    return pl.BlockSpec(shape, lambda *_: (0,) * nd, pipeline_mode=pl.Buffered(1))


def _mix_ffn_kernel(h_ref, y0_ref, y1_ref, y2_ref, y3_ref, wout_ref, g1_ref, b1_ref,
                    wup_ref, cw_ref, cb_ref, wdn_ref, g_ref, b_ref, o_ref, prev_scr, *, alpha, d_ff):
    @pl.when(pl.program_id(1) == 0)
    def _():
        prev_scr[...] = jnp.zeros_like(prev_scr)

    acc = alpha * h_ref[...]
    for i, y_ref in enumerate((y0_ref, y1_ref, y2_ref, y3_ref)):
        acc = acc + jnp.dot(y_ref[...].astype(BF16), wout_ref[i * GROUP_W:(i + 1) * GROUP_W, :],
                            preferred_element_type=F32)
    h = _layer_norm(acc, g1_ref[...], b1_ref[...])

    rows = h.shape[0]
    up = jnp.dot(h.astype(BF16), wup_ref[...], preferred_element_type=F32)
    u, gate = up[:, :d_ff], up[:, d_ff:]
    prev = prev_scr[...]
    prev_scr[...] = u[rows - 8:rows]
    cw = cw_ref[...]
    conv = u * cw[2:3] + cb_ref[...]
    conv = conv + _shift_rows(u, prev, 1) * cw[1:2] + _shift_rows(u, prev, 2) * cw[0:1]
    act = (_gelu_tanh(conv) * gate).astype(BF16)
    down = jnp.dot(act, wdn_ref[...], preferred_element_type=F32)
    o_ref[...] = _layer_norm(alpha * h + down, g_ref[...], b_ref[...])


def _mix_ffn_call(h, ys, w_out, g1, b1, w_up, conv_w, conv_b, w_down, g2, b2, alpha, rows=512):
    bsz, seq, d = h.shape
    d_ff = w_down.shape[0]
    blk = lambda w: pl.BlockSpec((None, rows, w), lambda bi, i: (bi, i, 0))
    vec = _full_spec((1, d))
    return pl.pallas_call(
        functools.partial(_mix_ffn_kernel, alpha=alpha, d_ff=d_ff),
        grid=(bsz, seq // rows),
        in_specs=[blk(d), pl.BlockSpec((rows, GROUP_W), lambda bi, i: (i, bi))] + [blk(GROUP_W)] * 3
        + [_resident_spec(w_out.shape), vec, vec,
           _resident_spec(w_up.shape), _full_spec(conv_w.shape), _full_spec((1, d_ff)),
           _resident_spec(w_down.shape), vec, vec],
        out_specs=blk(d),
        out_shape=jax.ShapeDtypeStruct((bsz, seq, d), F32),
        scratch_shapes=[pltpu.VMEM((8, d_ff), F32)],
        compiler_params=_params(("arbitrary", "arbitrary")),
        name="mix_ffn_ln",
    )(h, *ys, w_out.astype(BF16), g1.reshape(1, d), b1.reshape(1, d),
      w_up.astype(BF16), conv_w, conv_b.reshape(1, d_ff), w_down.astype(BF16),
      g2.reshape(1, d), b2.reshape(1, d))


DEPTH = 2
DN_ALPHA = (2 * DEPTH) ** 0.25
IN_SIZES = (256, 1024, 768, 1024, 8)
GATE_PAD = 128
S5_STEPS = 128
ROW_TILE = 512


def kernel(x, ln_in_g, ln_in_b, w_in, s5_lambda_re, s5_lambda_im, s5_log_dt, s5_b_re, s5_b_im, s5_c_re, s5_c_im, s5_d, s5_w_glu, s5_b_glu, rw_mu, rw_w0, rw_w2, rw_a0, rw_a2, rw_g2, rw_k_k, rw_k_a, rw_r_k, rw_ln_g, rw_ln_b, ml_conv_w, ml_conv_b, ml_b_i, ml_b_f, ml_ln_g, w_out, ln1_g, ln1_b, ffn_w_up, ffn_conv_w, ffn_conv_b, ffn_w_down, ln2_g, ln2_b):
    bsz, seq, d = x.shape
    n = bsz * seq
    h = x.reshape(n, d)
    cuts = [0]
    for s in IN_SIZES:
        cuts.append(cuts[-1] + s)
    for l in range(DEPTH):
        ws = [w_in[l][:, cuts[i]:cuts[i + 1]].astype(BF16) for i in range(len(IN_SIZES))]
        ws[4] = jnp.pad(ws[4], ((0, 0), (0, GATE_PAD - IN_SIZES[4])))
        if l == 0:
            h, z_s5, z_rw, z_sb, z_ml, z_g = _inproj_call(h, ws, ROW_TILE, bsz, ln=(ln_in_g, ln_in_b))
        else:
            z_s5, z_rw, z_sb, z_ml, z_g = _inproj_call(h, ws, ROW_TILE, bsz)

        y_s5 = _s5_call(z_s5.reshape(n, GROUP_W), bsz, s5_lambda_re[l], s5_lambda_im[l], s5_log_dt[l],
                        s5_b_re[l], s5_b_im[l], s5_c_re[l], s5_c_im[l], s5_d[l], s5_w_glu[l], s5_b_glu[l],
                        S5_STEPS).reshape(seq, bsz * GROUP_W)
        y_rw = _rwkv_call(z_rw.reshape(bsz, seq, -1), rw_mu[l], rw_w0[l], rw_w2[l], rw_a0[l], rw_a2[l],
                          rw_g2[l], rw_k_k[l], rw_k_a[l], rw_r_k[l], rw_ln_g[l], rw_ln_b[l])
        y_sb = _sb_call(z_sb.reshape(bsz, seq, -1))
        y_ml = _mlstm_call(z_ml.reshape(bsz, seq, -1), z_g.reshape(bsz, seq, -1), ml_conv_w[l],
                           ml_conv_b[l], ml_b_i[l], ml_b_f[l], ml_ln_g[l])
        h = _mix_ffn_call(h.reshape(bsz, seq, d), [y_s5, y_rw, y_sb, y_ml], w_out[l], ln1_g[l], ln1_b[l],
                          ffn_w_up[l], ffn_conv_w[l], ffn_conv_b[l], ffn_w_down[l], ln2_g[l], ln2_b[l],
                          DN_ALPHA).reshape(n, d)
    return h.reshape(bsz, seq, d)
```

```python
import functools
import math

import jax
import jax.numpy as jnp
from jax import lax
from jax.experimental import pallas as pl
from jax.experimental.pallas import tpu as pltpu

F32 = jnp.float32
BF16 = jnp.bfloat16

GROUP_W = 256
HEAD_DIM = 64
N_HEADS = GROUP_W // HEAD_DIM
CHUNK = 64
S5_G, S5_P, S5_CH = 16, 64, 16
S5_STATE = S5_G * S5_P
RW_LORA = 128
LN_EPS = 1e-5
RW_GN_EPS = 64e-5
SB_SKIP_LOG = 110.0
VMEM_LIMIT = 56 * 1024 * 1024


def _bdot(a, b):
    return jnp.dot(a.astype(BF16), b.astype(BF16), preferred_element_type=F32)


def _bdot_nt(a, b):
    return lax.dot_general(a.astype(BF16), b.astype(BF16), (((1,), (1,)), ((), ())),
                           preferred_element_type=F32)


def _bdot_tn(a, b):
    return lax.dot_general(a.astype(BF16), b.astype(BF16), (((0,), (0,)), ((), ())),
                           preferred_element_type=F32)


def _split3(x):
    hi = x.astype(BF16)
    r1 = x - hi.astype(F32)
    mid = r1.astype(BF16)
    lo = (r1 - mid.astype(F32)).astype(BF16)
    return hi, mid, lo


def _xdot_left(sel, a):
    hi, mid, lo = _split3(a)
    s = sel.astype(BF16)
    out = jnp.dot(s, lo, preferred_element_type=F32)
    out = out + jnp.dot(s, mid, preferred_element_type=F32)
    return out + jnp.dot(s, hi, preferred_element_type=F32)


def _split2(x):
    hi = x.astype(BF16)
    return hi, (x - hi.astype(F32)).astype(BF16)


def _sdot(a, sel):
    hi, lo = _split2(a)
    s = sel.astype(BF16)
    return jnp.dot(lo, s, preferred_element_type=F32) + jnp.dot(hi, s, preferred_element_type=F32)


def _sdot_left(sel, a):
    hi, lo = _split2(a)
    s = sel.astype(BF16)
    return jnp.dot(s, lo, preferred_element_type=F32) + jnp.dot(s, hi, preferred_element_type=F32)


def _sigmoid(x):
    return 1.0 / (1.0 + jnp.exp(-x))


def _softplus(x):
    return jnp.maximum(x, 0.0) + jnp.log(1.0 + jnp.exp(-jnp.abs(x)))


def _gelu_tanh(x):
    c = math.sqrt(2.0 / math.pi)
    return 0.5 * x * (1.0 + jnp.tanh(c * (x + 0.044715 * (x * x * x))))


def _layer_norm(x, g, b):
    mu = jnp.mean(x, axis=-1, keepdims=True)
    xc = x - mu
    var = jnp.mean(xc * xc, axis=-1, keepdims=True)
    return xc * lax.rsqrt(var + LN_EPS) * g + b


def _iota2(shape, dim):
    return lax.broadcasted_iota(jnp.int32, shape, dim)


def _head_ones():
    r = _iota2((GROUP_W, GROUP_W), 0) // HEAD_DIM
    c = _iota2((GROUP_W, GROUP_W), 1) // HEAD_DIM
    return (r == c).astype(F32)


def _stack_heads(x):
    lane_head = _iota2(x.shape, 1) // HEAD_DIM
    return jnp.concatenate([jnp.where(lane_head == h, x, 0.0) for h in range(N_HEADS)], axis=0)


def _shift_rows(x, prev, n):
    ext = jnp.concatenate([prev, x], axis=0)
    return ext[8 - n:8 - n + x.shape[0]]


def _full_spec(shape):
    nd = len(shape)
    return pl.BlockSpec(shape, lambda *_: (0,) * nd)


def _params(sem):
    return pltpu.CompilerParams(dimension_semantics=sem, vmem_limit_bytes=VMEM_LIMIT)


def _inproj_kernel(h_ref, *refs):
    nw = len(refs) // 2
    hb = h_ref[...].astype(BF16)
    for w_ref, o_ref in zip(refs[:nw], refs[nw:]):
        o_ref[...] = jnp.dot(hb, w_ref[...], preferred_element_type=F32)


def _ln_inproj_kernel(x_ref, g_ref, b_ref, *refs):
    h = _layer_norm(x_ref[...], g_ref[...], b_ref[...])
    refs[len(refs) // 2][...] = h
    hb = h.astype(BF16)
    nw = len(refs) // 2
    for w_ref, o_ref in zip(refs[:nw], refs[nw + 1:]):
        o_ref[...] = jnp.dot(hb, w_ref[...], preferred_element_type=F32)


def _time_major_spec(rows, seq):
    tiles = seq // rows
    return pl.BlockSpec((rows, GROUP_W), lambda i: (i % tiles, i // tiles))


def _inproj_call(h2d, weights, rows, bsz, ln=None):
    n, d = h2d.shape
    seq = n // bsz
    row_spec = lambda w: pl.BlockSpec((rows, w), lambda i: (i, 0))
    out_specs = [row_spec(w.shape[1]) for w in weights]
    out_shape = [jax.ShapeDtypeStruct((n, w.shape[1]), F32) for w in weights]
    out_specs[0] = _time_major_spec(rows, seq)
    out_shape[0] = jax.ShapeDtypeStruct((seq, bsz * GROUP_W), F32)
    in_specs = [row_spec(d)]
    args = [h2d]
    if ln is not None:
        in_specs += [_full_spec((1, d)), _full_spec((1, d))]
        args += [ln[0].reshape(1, d), ln[1].reshape(1, d)]
        out_specs = [row_spec(d)] + out_specs
        out_shape = [jax.ShapeDtypeStruct((n, d), F32)] + out_shape
    return pl.pallas_call(
        _inproj_kernel if ln is None else _ln_inproj_kernel,
        grid=(n // rows,),
        in_specs=in_specs + [_full_spec(w.shape) for w in weights],
        out_specs=out_specs,
        out_shape=out_shape,
        compiler_params=_params(("parallel",)),
        name="in_proj" if ln is None else "ln_in_proj",
    )(*args, *weights)


def _s5_kernel(u_ref, lre_ref, lim_ref, dt_ref, bmat_ref, cmat_ref, d_ref, wglu_ref, bglu_ref,
               y_ref, bu_scr, s_scr, st_scr, *, steps):
    @pl.when(pl.program_id(0) == 0)
    def _():
        st_scr[...] = jnp.zeros_like(st_scr)

    lr, li, dt = lre_ref[...], lim_ref[...], jnp.exp(dt_ref[...])
    mag = jnp.exp(lr * dt)
    ab_re, ab_im = mag * jnp.cos(li * dt), mag * jnp.sin(li * dt)
    den = lr * lr + li * li
    z_re = ((ab_re - 1.0) * lr + ab_im * li) / den
    z_im = (ab_im * lr - (ab_re - 1.0) * li) / den

    bsz = st_scr.shape[0]
    u = u_ref[...]
    p = _bdot(u, bmat_ref[...])
    p_re, p_im = p[:, :S5_STATE], p[:, S5_STATE:]
    bu_scr[:, :S5_STATE] = z_re * p_re - z_im * p_im
    bu_scr[:, S5_STATE:] = z_re * p_im + z_im * p_re

    a_re = jnp.broadcast_to(ab_re, (bsz, S5_STATE))
    a_im = jnp.broadcast_to(ab_im, (bsz, S5_STATE))

    def step(t, carry):
        s_re, s_im = carry
        row = pl.multiple_of(t * bsz, bsz)
        n_re = a_re * s_re - a_im * s_im + bu_scr[pl.ds(row, bsz), :S5_STATE]
        n_im = a_re * s_im + a_im * s_re + bu_scr[pl.ds(row, bsz), S5_STATE:]
        s_scr[pl.ds(row, bsz), :S5_STATE] = n_re
        s_scr[pl.ds(row, bsz), S5_STATE:] = n_im
        return n_re, n_im

    s_re, s_im = lax.fori_loop(0, steps, step, (st_scr[:, :S5_STATE], st_scr[:, S5_STATE:]), unroll=2)
    st_scr[:, :S5_STATE] = s_re
    st_scr[:, S5_STATE:] = s_im

    y = _bdot(s_scr[...], cmat_ref[...]) + d_ref[...] * u
    y = _gelu_tanh(y)
    y_ref[...] = y * _sigmoid(_bdot(y, wglu_ref[...]) + bglu_ref[...])


def _s5_call(u_tm, bsz, lam_re, lam_im, log_dt, b_re, b_im, c_re, c_im, d, w_glu, b_glu, steps):
    n = u_tm.shape[0]
    gp = S5_STATE
    eye_g = jnp.eye(S5_G, dtype=F32)
    b_bd = lambda b: jnp.einsum("gpc,gh->gchp", b, eye_g).reshape(GROUP_W, gp)
    c_bd = lambda c: jnp.einsum("gcp,gh->gphc", c, eye_g).reshape(gp, GROUP_W)
    bmat = jnp.concatenate([b_bd(b_re), b_bd(b_im)], axis=1).astype(BF16)
    cmat = jnp.concatenate([c_bd(c_re), -c_bd(c_im)], axis=0).astype(BF16)
    dt = jnp.repeat(log_dt, S5_P).reshape(1, gp)
    rows = steps * bsz
    kern = functools.partial(_s5_kernel, steps=steps)
    blk = pl.BlockSpec((rows, GROUP_W), lambda i: (i, 0))
    return pl.pallas_call(
        kern,
        grid=(n // rows,),
        in_specs=[blk,
                  _full_spec((1, gp)), _full_spec((1, gp)), _full_spec((1, gp)),
                  _full_spec((GROUP_W, 2 * gp)), _full_spec((2 * gp, GROUP_W)),
                  _full_spec((1, GROUP_W)), _full_spec((GROUP_W, GROUP_W)), _full_spec((1, GROUP_W))],
        out_specs=blk,
        out_shape=jax.ShapeDtypeStruct((n, GROUP_W), F32),
        scratch_shapes=[pltpu.VMEM((rows, 2 * gp), F32), pltpu.VMEM((rows, 2 * gp), F32),
                        pltpu.VMEM((bsz, 2 * gp), F32)],
        compiler_params=_params(("arbitrary",)),
        name="s5_mixer",
    )(u_tm, lam_re.reshape(1, gp), lam_im.reshape(1, gp), dt, bmat, cmat,
      d.reshape(1, GROUP_W), w_glu.astype(BF16), b_glu.reshape(1, GROUP_W))


def _hdot(a_c, b):
    return _bdot(a_c, _stack_heads(b))


def _unit_lower_inverse_apply(a, rhs):
    ti, si = _iota2(a.shape, 0), _iota2(a.shape, 1) % HEAD_DIM
    eye = (ti == si).astype(F32)
    a_d = jnp.where((ti // 16) == (si // 16), a, 0.0)
    a_o = a - a_d
    x = eye - a_d
    p = _hdot(a_d, a_d)
    x = x + _hdot(x, p)
    p = _hdot(p, p)
    x = x + _hdot(x, p)
    p = _hdot(p, p)
    d_inv = x + _hdot(x, p)
    nn = _hdot(d_inv, a_o)
    w = eye - nn
    w = w + _hdot(w, _hdot(nn, nn))
    return _hdot(w, _hdot(d_inv, rhs))


def _rwkv_kernel(z_ref, mu_ref, w0_ref, w2_ref, a0_ref, a2_ref, g2_ref, kk_ref, ka_ref, rk_ref,
                 lng_ref, lnb_ref, y_ref, prev_scr, n_scr):
    @pl.when(pl.program_id(1) == 0)
    def _():
        prev_scr[...] = jnp.zeros_like(prev_scr)
        n_scr[...] = jnp.zeros_like(n_scr)

    params = tuple(ref[...] for ref in (mu_ref, w0_ref, w2_ref, a0_ref, a2_ref, g2_ref, kk_ref, ka_ref,
                                        rk_ref, lng_ref, lnb_ref))
    chunk = jax.vmap(lambda z, prev, n0, tri: _rwkv_chunk(z, prev, n0, tri, params))
    tri = _batched_tri(z_ref.shape[0])
    prev, state = prev_scr[...], n_scr[...]
    for c in range(z_ref.shape[1] // CHUNK):
        rows = slice(c * CHUNK, (c + 1) * CHUNK)
        y_ref[:, rows, :], state = chunk(z_ref[:, rows, :], prev, state, tri)
        prev = z_ref[:, (c + 1) * CHUNK - 8:(c + 1) * CHUNK, :]
    prev_scr[...] = prev
    n_scr[...] = state


def _batched_tri(nb):
    shape = (nb, CHUNK, CHUNK)
    return (_iota2(shape, 1) >= _iota2(shape, 2)).astype(F32)


def _rwkv_chunk(z, prev, n0, tri, params):
    mu, w0, w2, a0, a2, g2, k_k, k_a, r_k, ln_g, ln_b = params
    L = CHUNK
    zs = _shift_rows(z, prev, 1)
    zm = z + (zs - z) * mu
    r, k, v = zm[:, 0:256], zm[:, 256:512], zm[:, 512:768]
    slab, xg = zm[:, 768:768 + RW_LORA], zm[:, 896:1024]

    w_raw = -_softplus(-(w0 + _bdot(jnp.tanh(slab), w2))) - 0.5
    lw = -jnp.exp(w_raw)
    a = _sigmoid(a0 + _bdot(slab, a2))
    g = _bdot(_sigmoid(xg), g2)

    hones = _head_ones()
    kk = k * k_k
    kt = k * (1.0 + (a - 1.0) * k_a)
    sums = _bdot(jnp.concatenate([kk * kk, r * kt * r_k], axis=0), hones)
    kkn = kk / jnp.maximum(jnp.sqrt(sums[:L]), 1e-12)
    bonus = sums[L:]
    beta = kkn * a

    cum = _sdot_left(tri, lw)
    cum_last = cum[L - 1:L, :]
    e_neg = jnp.exp(-cum)
    e_tail = jnp.exp(cum_last - cum)
    qr = jnp.concatenate([kkn * jnp.exp(cum - lw), r * jnp.exp(cum)], axis=0)
    kkb_s = jnp.concatenate([_stack_heads(kt * e_neg), _stack_heads(beta * e_neg)], axis=0)
    prod = _bdot_nt(qr, kkb_s)
    ti, si = _iota2((L, GROUP_W), 0), _iota2((L, GROUP_W), 1) % HEAD_DIM
    strict, incl = ti > si, ti >= si
    a_k = jnp.where(strict, prod[:L, :GROUP_W], 0.0)
    a_b = jnp.where(strict, prod[:L, GROUP_W:], 0.0)
    b_k = jnp.where(incl, prod[L:, :GROUP_W], 0.0)
    b_b = jnp.where(incl, prod[L:, GROUP_W:], 0.0)

    reads = _bdot_nt(qr, n0)
    v_s = _stack_heads(v)
    u_w = _unit_lower_inverse_apply(a_b, reads[:L] + _bdot(a_k, v_s))
    y = reads[L:] + _bdot(b_k, v_s) - _hdot(b_b, u_w)

    same_head = (_iota2((GROUP_W, GROUP_W), 0) // HEAD_DIM) == (_iota2((GROUP_W, GROUP_W), 1) // HEAD_DIM)
    upd = _bdot_tn(jnp.concatenate([v, -u_w], axis=0),
                   jnp.concatenate([kt * e_tail, beta * e_tail], axis=0))
    n_new = n0 * jnp.exp(cum_last) + jnp.where(same_head, upd, 0.0)

    inv_hd = 1.0 / HEAD_DIM
    mean = _sdot(y, hones) * inv_hd
    yc = y - mean
    var = _sdot(yc * yc, hones) * inv_hd
    yn = yc * lax.rsqrt(var + RW_GN_EPS) * ln_g + ln_b
    return (yn + bonus * v) * g, n_new


def _rwkv_call(z_rw, mu, w0, w2, a0, a2, g2, k_k, k_a, r_k, ln_g, ln_b, nb=8, chunks=4):
    bsz, seq, width = z_rw.shape
    rows = chunks * CHUNK
    row = lambda t: t.reshape(1, -1)
    zeros = jnp.zeros((RW_LORA // 2, GROUP_W), F32)
    w2p = jnp.concatenate([w2, zeros], axis=0).astype(BF16)
    a2p = jnp.concatenate([zeros, a2], axis=0).astype(BF16)
    vec = _full_spec((1, GROUP_W))
    mat = _full_spec((RW_LORA, GROUP_W))
    return pl.pallas_call(
        _rwkv_kernel,
        grid=(bsz // nb, seq // rows),
        in_specs=[pl.BlockSpec((nb, rows, width), lambda b, i: (b, i, 0)),
                  _full_spec((1, width)), vec, mat, vec, mat, mat, vec, vec, vec, vec, vec],
        out_specs=pl.BlockSpec((nb, rows, GROUP_W), lambda b, i: (b, i, 0)),
        out_shape=jax.ShapeDtypeStruct((bsz, seq, GROUP_W), F32),
        scratch_shapes=[pltpu.VMEM((nb, 8, width), F32), pltpu.VMEM((nb, GROUP_W, GROUP_W), F32)],
        compiler_params=_params(("arbitrary", "arbitrary")),
        name="rwkv7_mixer",
    )(z_rw, row(mu), row(w0), w2p, row(a0), a2p, g2.astype(BF16), row(k_k), row(k_a), row(r_k),
      row(ln_g), row(ln_b))


def _sb_kernel(q_ref, k_ref, v_ref, o_ref, *, blk):
    qi = pl.program_id(1)
    n4 = N_HEADS * blk
    q_s = _stack_heads(q_ref[...] * (HEAD_DIM ** -0.5)).astype(BF16)
    later = (_iota2((blk, blk), 0) > _iota2((blk, blk), 1)).astype(F32)

    def key_blocks(kbs, carry, acc, diagonal):
        starts = [pl.multiple_of(kb * blk, blk) for kb in kbs]
        logits = [_bdot_nt(q_s, k_ref[pl.ds(s, blk), :]) for s in starts]
        sp = [_softplus(l) for l in logits]
        sp_m = list(sp)
        if diagonal:
            mask = _iota2((n4, blk), 1) < _iota2((n4, blk), 0) % blk
            sp_m[0] = jnp.where(mask, sp[0], 0.0)
        within = [_sdot(s, later) for s in sp_m]
        for i in range(len(kbs)):
            wgt = jnp.exp(logits[i] - sp[i] - (carry + within[i]))
            if diagonal and i == 0:
                wgt = jnp.where(mask, wgt, 0.0)
            wgt_w = jnp.concatenate([wgt[h * blk:(h + 1) * blk] for h in range(N_HEADS)], axis=1)
            acc = acc + _bdot(wgt_w, _stack_heads(v_ref[pl.ds(starts[i], blk), :]))
            carry = carry + jnp.sum(sp_m[i], axis=1, keepdims=True)
        return carry, acc

    def cond(state):
        kb, carry, _ = state
        return jnp.logical_and(kb >= 0, jnp.min(carry) < SB_SKIP_LOG)

    def body(state):
        kb, carry, acc = state
        carry, acc = key_blocks([kb], carry, acc, diagonal=False)
        return kb - 1, carry, acc

    zero = (jnp.zeros((n4, 1), F32), jnp.zeros((blk, GROUP_W), F32))
    carry, acc = lax.cond(qi > 0,
                          lambda: key_blocks([qi, qi - 1], *zero, diagonal=True),
                          lambda: key_blocks([qi], *zero, diagonal=True))
    _, _, acc = lax.while_loop(cond, body, (qi - 2, carry, acc))
    o_ref[...] = acc


def _sb_call(z_sb, blk=256):
    bsz, seq, _ = z_sb.shape
    kern = functools.partial(_sb_kernel, blk=blk)
    return pl.pallas_call(
        kern,
        grid=(bsz, seq // blk),
        in_specs=[pl.BlockSpec((None, blk, GROUP_W), lambda b, i: (b, i, 0)),
                  pl.BlockSpec((None, seq, GROUP_W), lambda b, i: (b, 0, 1)),
                  pl.BlockSpec((None, seq, GROUP_W), lambda b, i: (b, 0, 2))],
        out_specs=pl.BlockSpec((None, blk, GROUP_W), lambda b, i: (b, i, 0)),
        out_shape=jax.ShapeDtypeStruct((bsz, seq, GROUP_W), F32),
        compiler_params=_params(("parallel", "arbitrary")),
        name="stickbreak_attn",
    )(z_sb, z_sb, z_sb)


def _mlstm_kernel(z_ref, g_ref, cw_ref, cb_ref, bi_ref, bf_ref, lng_ref, y_ref,
                  prev_scr, c_scr, n_scr, m_scr):
    @pl.when(pl.program_id(1) == 0)
    def _():
        prev_scr[...] = jnp.zeros_like(prev_scr)
        c_scr[...] = jnp.zeros_like(c_scr)
        n_scr[...] = jnp.zeros_like(n_scr)
        m_scr[...] = jnp.full(m_scr.shape, -jnp.inf, F32)

    params = tuple(ref[...] for ref in (cw_ref, cb_ref, bi_ref, bf_ref, lng_ref))
    chunk = jax.vmap(lambda z, gz, prev, state, tri: _mlstm_chunk(z, gz, prev, state, tri, params))
    tri = _batched_tri(z_ref.shape[0])
    prev, state = prev_scr[...], (c_scr[...], n_scr[:, 0:1, :], m_scr[:, 0:1, :])
    for c in range(z_ref.shape[1] // CHUNK):
        rows = slice(c * CHUNK, (c + 1) * CHUNK)
        y_ref[:, rows, :], state = chunk(z_ref[:, rows, :], g_ref[:, rows, :], prev, state, tri)
        prev = z_ref[:, (c + 1) * CHUNK - 8:(c + 1) * CHUNK, 0:2 * GROUP_W]
    prev_scr[...] = prev
    c_scr[...] = state[0]
    n_scr[...] = jnp.broadcast_to(state[1], n_scr.shape)
    m_scr[...] = jnp.broadcast_to(state[2], m_scr.shape)


def _mlstm_chunk(z, gz, prev, state, tri, params):
    cw, cb, b_i, b_f, ln_g = params
    c_prev, n_prev, m_prev = state
    L = CHUNK
    qk_raw = z[:, 0:2 * GROUP_W]
    conv = qk_raw * cw[3:4] + cb
    for j in range(3):
        conv = conv + _shift_rows(qk_raw, prev, 3 - j) * cw[j:j + 1]
    qk = conv * _sigmoid(conv)
    q, k = qk[:, :GROUP_W], qk[:, GROUP_W:] * (HEAD_DIM ** -0.5)
    v, o = z[:, 2 * GROUP_W:3 * GROUP_W], z[:, 3 * GROUP_W:]

    gate_col = _iota2((gz.shape[1], GROUP_W), 0)
    gate_head = _iota2((gz.shape[1], GROUP_W), 1) // HEAD_DIM
    ig = _sdot(gz, (gate_col == gate_head).astype(F32)) + b_i
    log_f = -_softplus(-(gz + b_f))
    log_f = _sdot(log_f, (gate_col == gate_head + N_HEADS).astype(F32))

    b = _xdot_left(tri, log_f)
    b_last = b[L - 1:L, :]

    gl = b_last - b + ig
    m_loc = jnp.max(gl, axis=0, keepdims=True)
    kw = k * jnp.exp(gl - m_loc)
    same_head = (_iota2((GROUP_W, GROUP_W), 0) // HEAD_DIM) == (_iota2((GROUP_W, GROUP_W), 1) // HEAD_DIM)
    c_loc = jnp.where(same_head, _bdot_tn(kw, v), 0.0)
    n_loc = jnp.sum(kw, axis=0, keepdims=True)
    m_new = jnp.maximum(b_last + m_prev, m_loc)
    s_old, s_new = jnp.exp(b_last + m_prev - m_new), jnp.exp(m_loc - m_new)
    new_state = (s_old * c_prev + s_new * c_loc, s_old * n_prev + s_new * n_loc, m_new)

    lane = _iota2((L, GROUP_W), 1)
    lane_head, lane_s = lane // HEAD_DIM, lane % HEAD_DIM
    row_l = _iota2((L, GROUP_W), 0)
    x = ig - b
    x_row = _xdot_left(jnp.maximum(tri, 1.0), jnp.where(row_l == lane_s, x, 0.0))
    log_d = b + x_row
    causal = lane_s <= row_l
    m_intra = jnp.full((L, GROUP_W), -jnp.inf, F32)
    for h in range(N_HEADS):
        sel = lane_head == h
        mh = jnp.max(jnp.where(jnp.logical_and(sel, causal), log_d, -jnp.inf), axis=1, keepdims=True)
        m_intra = jnp.where(sel, mh, m_intra)
    log_inter = b + m_prev
    m_t = jnp.maximum(log_inter, m_intra)
    w_intra = jnp.where(causal, jnp.exp(log_d - m_t), 0.0)
    w_inter = jnp.exp(log_inter - m_t)

    hones = _head_ones()
    s_mat = _bdot_nt(q, _stack_heads(k)) * w_intra
    num = w_inter * _bdot(q, c_prev) + _bdot(s_mat, _stack_heads(v))
    den = w_inter * _bdot(q * n_prev, hones) + _bdot(s_mat, hones)
    hval = num / jnp.maximum(jnp.abs(den), jnp.exp(-m_t))

    inv_hd = 1.0 / HEAD_DIM
    mean = _sdot(hval, hones) * inv_hd
    hc = hval - mean
    var = _sdot(hc * hc, hones) * inv_hd
    return _sigmoid(o) * (hc * lax.rsqrt(var + LN_EPS) * ln_g), new_state


def _mlstm_call(z_ml, z_gate, conv_w, conv_b, b_i, b_f, ln_g, nb=8, chunks=4):
    bsz, seq, width = z_ml.shape
    gw = z_gate.shape[2]
    rows = chunks * CHUNK
    lanes = lambda t: jnp.repeat(t, HEAD_DIM).reshape(1, GROUP_W)
    b_f_cols = jnp.pad(b_f, (N_HEADS, gw - 2 * N_HEADS)).reshape(1, gw)
    vec = _full_spec((1, GROUP_W))
    return pl.pallas_call(
        _mlstm_kernel,
        grid=(bsz // nb, seq // rows),
        in_specs=[pl.BlockSpec((nb, rows, width), lambda b, i: (b, i, 0)),
                  pl.BlockSpec((nb, rows, gw), lambda b, i: (b, i, 0)),
                  _full_spec(conv_w.shape), _full_spec((1, 2 * GROUP_W)), vec, _full_spec((1, gw)), vec],
        out_specs=pl.BlockSpec((nb, rows, GROUP_W), lambda b, i: (b, i, 0)),
        out_shape=jax.ShapeDtypeStruct((bsz, seq, GROUP_W), F32),
        scratch_shapes=[pltpu.VMEM((nb, 8, 2 * GROUP_W), F32), pltpu.VMEM((nb, GROUP_W, GROUP_W), F32),
                        pltpu.VMEM((nb, 8, GROUP_W), F32), pltpu.VMEM((nb, 8, GROUP_W), F32)],
        compiler_params=_params(("arbitrary", "arbitrary")),
        name="mlstm_mixer",
    )(z_ml, z_gate, conv_w, conv_b.reshape(1, -1), lanes(b_i), b_f_cols, ln_g.reshape(1, -1))


def _resident_spec(shape):
    nd = len(shape)
    return pl.BlockSpec(shape, lambda *_: (0,) * nd, pipeline_mode=pl.Buffered(1))


def _mix_ffn_kernel(h_ref, y0_ref, y1_ref, y2_ref, y3_ref, wout_ref, g1_ref, b1_ref,
                    wup_ref, cw_ref, cb_ref, wdn_ref, g_ref, b_ref, o_ref, prev_scr, *, alpha, d_ff):
    @pl.when(pl.program_id(1) == 0)
    def _():
        prev_scr[...] = jnp.zeros_like(prev_scr)

    acc = alpha * h_ref[...]
    for i, y_ref in enumerate((y0_ref, y1_ref, y2_ref, y3_ref)):
        acc = acc + jnp.dot(y_ref[...].astype(BF16), wout_ref[i * GROUP_W:(i + 1) * GROUP_W, :],
                            preferred_element_type=F32)
    h = _layer_norm(acc, g1_ref[...], b1_ref[...])

    rows = h.shape[0]
    up = jnp.dot(h.astype(BF16), wup_ref[...], preferred_element_type=F32)
    u, gate = up[:, :d_ff], up[:, d_ff:]
    prev = prev_scr[...]
    prev_scr[...] = u[rows - 8:rows]
    cw = cw_ref[...]
    conv = u * cw[2:3] + cb_ref[...]
    conv = conv + _shift_rows(u, prev, 1) * cw[1:2] + _shift_rows(u, prev, 2) * cw[0:1]
    act = (_gelu_tanh(conv) * gate).astype(BF16)
    down = jnp.dot(act, wdn_ref[...], preferred_element_type=F32)
    o_ref[...] = _layer_norm(alpha * h + down, g_ref[...], b_ref[...])


def _mix_ffn_call(h, ys, w_out, g1, b1, w_up, conv_w, conv_b, w_down, g2, b2, alpha, rows=512):
    bsz, seq, d = h.shape
    d_ff = w_down.shape[0]
    blk = lambda w: pl.BlockSpec((None, rows, w), lambda bi, i: (bi, i, 0))
    vec = _full_spec((1, d))
    return pl.pallas_call(
        functools.partial(_mix_ffn_kernel, alpha=alpha, d_ff=d_ff),
        grid=(bsz, seq // rows),
        in_specs=[blk(d), pl.BlockSpec((rows, GROUP_W), lambda bi, i: (i, bi))] + [blk(GROUP_W)] * 3
        + [_resident_spec(w_out.shape), vec, vec,
           _resident_spec(w_up.shape), _full_spec(conv_w.shape), _full_spec((1, d_ff)),
           _resident_spec(w_down.shape), vec, vec],
        out_specs=blk(d),
        out_shape=jax.ShapeDtypeStruct((bsz, seq, d), F32),
        scratch_shapes=[pltpu.VMEM((8, d_ff), F32)],
        compiler_params=_params(("arbitrary", "arbitrary")),
        name="mix_ffn_ln",
    )(h, *ys, w_out.astype(BF16), g1.reshape(1, d), b1.reshape(1, d),
      w_up.astype(BF16), conv_w, conv_b.reshape(1, d_ff), w_down.astype(BF16),
      g2.reshape(1, d), b2.reshape(1, d))


DEPTH = 2
DN_ALPHA = (2 * DEPTH) ** 0.25
IN_SIZES = (256, 1024, 768, 1024, 8)
GATE_PAD = 128
S5_STEPS = 128
ROW_TILE = 512


def kernel(x, ln_in_g, ln_in_b, w_in, s5_lambda_re, s5_lambda_im, s5_log_dt, s5_b_re, s5_b_im, s5_c_re, s5_c_im, s5_d, s5_w_glu, s5_b_glu, rw_mu, rw_w0, rw_w2, rw_a0, rw_a2, rw_g2, rw_k_k, rw_k_a, rw_r_k, rw_ln_g, rw_ln_b, ml_conv_w, ml_conv_b, ml_b_i, ml_b_f, ml_ln_g, w_out, ln1_g, ln1_b, ffn_w_up, ffn_conv_w, ffn_conv_b, ffn_w_down, ln2_g, ln2_b):
    bsz, seq, d = x.shape
    n = bsz * seq
    h = x.reshape(n, d)
    cuts = [0]
    for s in IN_SIZES:
        cuts.append(cuts[-1] + s)
    for l in range(DEPTH):
        ws = [w_in[l][:, cuts[i]:cuts[i + 1]].astype(BF16) for i in range(len(IN_SIZES))]
        ws[4] = jnp.pad(ws[4], ((0, 0), (0, GATE_PAD - IN_SIZES[4])))
        if l == 0:
            h, z_s5, z_rw, z_sb, z_ml, z_g = _inproj_call(h, ws, ROW_TILE, bsz, ln=(ln_in_g, ln_in_b))
        else:
            z_s5, z_rw, z_sb, z_ml, z_g = _inproj_call(h, ws, ROW_TILE, bsz)

        y_s5 = _s5_call(z_s5.reshape(n, GROUP_W), bsz, s5_lambda_re[l], s5_lambda_im[l], s5_log_dt[l],
                        s5_b_re[l], s5_b_im[l], s5_c_re[l], s5_c_im[l], s5_d[l], s5_w_glu[l], s5_b_glu[l],
                        S5_STEPS).reshape(seq, bsz * GROUP_W)
        y_rw = _rwkv_call(z_rw.reshape(bsz, seq, -1), rw_mu[l], rw_w0[l], rw_w2[l], rw_a0[l], rw_a2[l],
                          rw_g2[l], rw_k_k[l], rw_k_a[l], rw_r_k[l], rw_ln_g[l], rw_ln_b[l])
        y_sb = _sb_call(z_sb.reshape(bsz, seq, -1))
        y_ml = _mlstm_call(z_ml.reshape(bsz, seq, -1), z_g.reshape(bsz, seq, -1), ml_conv_w[l],
                           ml_conv_b[l], ml_b_i[l], ml_b_f[l], ml_ln_g[l])
        h = _mix_ffn_call(h.reshape(bsz, seq, d), [y_s5, y_rw, y_sb, y_ml], w_out[l], ln1_g[l], ln1_b[l],
                          ffn_w_up[l], ffn_conv_w[l], ffn_conv_b[l], ffn_w_down[l], ln2_g[l], ln2_b[l],
                          DN_ALPHA).reshape(n, d)
    return h.reshape(bsz, seq, d)
```

```python
import functools
import math

import jax
import jax.numpy as jnp
from jax import lax
from jax.experimental import pallas as pl
from jax.experimental.pallas import tpu as pltpu

F32 = jnp.float32
BF16 = jnp.bfloat16

GROUP_W = 256
HEAD_DIM = 64
N_HEADS = GROUP_W // HEAD_DIM
CHUNK = 64
S5_G, S5_P, S5_CH = 16, 64, 16
S5_STATE = S5_G * S5_P
RW_LORA = 128
LN_EPS = 1e-5
RW_GN_EPS = 64e-5
SB_SKIP_LOG = 110.0
VMEM_LIMIT = 56 * 1024 * 1024


def _bdot(a, b):
    return jnp.dot(a.astype(BF16), b.astype(BF16), preferred_element_type=F32)


def _bdot_nt(a, b):
    return lax.dot_general(a.astype(BF16), b.astype(BF16), (((1,), (1,)), ((), ())),
                           preferred_element_type=F32)


def _bdot_tn(a, b):
    return lax.dot_general(a.astype(BF16), b.astype(BF16), (((0,), (0,)), ((), ())),
                           preferred_element_type=F32)


def _split3(x):
    hi = x.astype(BF16)
    r1 = x - hi.astype(F32)
    mid = r1.astype(BF16)
    lo = (r1 - mid.astype(F32)).astype(BF16)
    return hi, mid, lo


def _xdot_left(sel, a):
    hi, mid, lo = _split3(a)
    s = sel.astype(BF16)
    out = jnp.dot(s, lo, preferred_element_type=F32)
    out = out + jnp.dot(s, mid, preferred_element_type=F32)
    return out + jnp.dot(s, hi, preferred_element_type=F32)


def _split2(x):
    hi = x.astype(BF16)
    return hi, (x - hi.astype(F32)).astype(BF16)


def _sdot(a, sel):
    hi, lo = _split2(a)
    s = sel.astype(BF16)
    return jnp.dot(lo, s, preferred_element_type=F32) + jnp.dot(hi, s, preferred_element_type=F32)


def _sdot_left(sel, a):
    hi, lo = _split2(a)
    s = sel.astype(BF16)
    return jnp.dot(s, lo, preferred_element_type=F32) + jnp.dot(s, hi, preferred_element_type=F32)


def _sigmoid(x):
    return 1.0 / (1.0 + jnp.exp(-x))


def _softplus(x):
    return jnp.maximum(x, 0.0) + jnp.log(1.0 + jnp.exp(-jnp.abs(x)))


def _gelu_tanh(x):
    c = math.sqrt(2.0 / math.pi)
    return 0.5 * x * (1.0 + jnp.tanh(c * (x + 0.044715 * (x * x * x))))


def _layer_norm(x, g, b):
    mu = jnp.mean(x, axis=-1, keepdims=True)
    xc = x - mu
    var = jnp.mean(xc * xc, axis=-1, keepdims=True)
    return xc * lax.rsqrt(var + LN_EPS) * g + b


def _iota2(shape, dim):
    return lax.broadcasted_iota(jnp.int32, shape, dim)


def _head_ones():
    r = _iota2((GROUP_W, GROUP_W), 0) // HEAD_DIM
    c = _iota2((GROUP_W, GROUP_W), 1) // HEAD_DIM
    return (r == c).astype(F32)


def _stack_heads(x):
    lane_head = _iota2(x.shape, 1) // HEAD_DIM
    return jnp.concatenate([jnp.where(lane_head == h, x, 0.0) for h in range(N_HEADS)], axis=0)


def _shift_rows(x, prev, n):
    ext = jnp.concatenate([prev, x], axis=0)
    return ext[8 - n:8 - n + x.shape[0]]


def _full_spec(shape):
    nd = len(shape)
    return pl.BlockSpec(shape, lambda *_: (0,) * nd)


def _params(sem):
    return pltpu.CompilerParams(dimension_semantics=sem, vmem_limit_bytes=VMEM_LIMIT)


def _inproj_kernel(*refs, with_ln):
    x_ref, wb_scr = refs[0], refs[-1]
    w_ref = refs[3] if with_ln else refs[1]
    outs = refs[(4 if with_ln else 2):-1]
    n_in = w_ref.shape[1]
    aligned = wb_scr.shape[1] - GATE_PAD

    @pl.when(pl.program_id(0) == 0)
    def _():
        wb_scr[:, :aligned] = w_ref[:, :aligned].astype(BF16)
        tail = w_ref[:, aligned:].astype(BF16)
        place = (_iota2((n_in - aligned, GATE_PAD), 0) == _iota2((n_in - aligned, GATE_PAD), 1)).astype(BF16)
        wb_scr[:, aligned:] = jnp.dot(tail, place, preferred_element_type=F32).astype(BF16)

    h = x_ref[...]
    if with_ln:
        h = _layer_norm(h, refs[1][...], refs[2][...])
        outs[0][...] = h
        outs = outs[1:]
    hb = h.astype(BF16)
    start = 0
    for o_ref in outs:
        width = o_ref.shape[-1]
        o_ref[...] = jnp.dot(hb, wb_scr[:, start:start + width], preferred_element_type=F32)
        start += width


def _time_major_spec(rows, seq):
    tiles = seq // rows
    return pl.BlockSpec((rows, GROUP_W), lambda i: (i % tiles, i // tiles))


def _inproj_call(h2d, w_in, layer, widths, rows, bsz, ln=None):
    n, d = h2d.shape
    seq = n // bsz
    n_in = w_in.shape[2]
    row_spec = lambda w: pl.BlockSpec((rows, w), lambda i: (i, 0))
    out_specs = [row_spec(w) for w in widths]
    out_shape = [jax.ShapeDtypeStruct((n, w), F32) for w in widths]
    out_specs[0] = _time_major_spec(rows, seq)
    out_shape[0] = jax.ShapeDtypeStruct((seq, bsz * GROUP_W), F32)
    in_specs = [row_spec(d)]
    args = [h2d]
    if ln is not None:
        in_specs += [_full_spec((1, d)), _full_spec((1, d))]
        args += [ln[0].reshape(1, d), ln[1].reshape(1, d)]
        out_specs = [row_spec(d)] + out_specs
        out_shape = [jax.ShapeDtypeStruct((n, d), F32)] + out_shape
    w_spec = pl.BlockSpec((None, d, n_in), lambda i: (layer, 0, 0), pipeline_mode=pl.Buffered(1))
    return pl.pallas_call(
        functools.partial(_inproj_kernel, with_ln=ln is not None),
        grid=(n // rows,),
        in_specs=in_specs + [w_spec],
        out_specs=out_specs,
        out_shape=out_shape,
        scratch_shapes=[pltpu.VMEM((d, sum(widths)), BF16)],
        compiler_params=_params(("arbitrary",)),
        name="in_proj" if ln is None else "ln_in_proj",
    )(*args, w_in)


def _s5_kernel(u_ref, lre_ref, lim_ref, dt_ref, bmat_ref, cmat_ref, d_ref, wglu_ref, bglu_ref,
               y_ref, bu_scr, s_scr, st_scr, *, steps):
    @pl.when(pl.program_id(0) == 0)
    def _():
        st_scr[...] = jnp.zeros_like(st_scr)

    lr, li, dt = lre_ref[...], lim_ref[...], jnp.exp(dt_ref[...])
    mag = jnp.exp(lr * dt)
    ab_re, ab_im = mag * jnp.cos(li * dt), mag * jnp.sin(li * dt)
    den = lr * lr + li * li
    z_re = ((ab_re - 1.0) * lr + ab_im * li) / den
    z_im = (ab_im * lr - (ab_re - 1.0) * li) / den

    bsz = st_scr.shape[0]
    u = u_ref[...]
    p = _bdot(u, bmat_ref[...])
    p_re, p_im = p[:, :S5_STATE], p[:, S5_STATE:]
    bu_scr[:, :S5_STATE] = z_re * p_re - z_im * p_im
    bu_scr[:, S5_STATE:] = z_re * p_im + z_im * p_re

    a_re = jnp.broadcast_to(ab_re, (bsz, S5_STATE))
    a_im = jnp.broadcast_to(ab_im, (bsz, S5_STATE))

    def step(t, carry):
        s_re, s_im = carry
        row = pl.multiple_of(t * bsz, bsz)
        n_re = a_re * s_re - a_im * s_im + bu_scr[pl.ds(row, bsz), :S5_STATE]
        n_im = a_re * s_im + a_im * s_re + bu_scr[pl.ds(row, bsz), S5_STATE:]
        s_scr[pl.ds(row, bsz), :S5_STATE] = n_re
        s_scr[pl.ds(row, bsz), S5_STATE:] = n_im
        return n_re, n_im

    s_re, s_im = lax.fori_loop(0, steps, step, (st_scr[:, :S5_STATE], st_scr[:, S5_STATE:]), unroll=2)
    st_scr[:, :S5_STATE] = s_re
    st_scr[:, S5_STATE:] = s_im

    y = _bdot(s_scr[...], cmat_ref[...]) + d_ref[...] * u
    y = _gelu_tanh(y)
    y_ref[...] = y * _sigmoid(_bdot(y, wglu_ref[...]) + bglu_ref[...])


def _s5_call(u_tm, bsz, lam_re, lam_im, log_dt, b_re, b_im, c_re, c_im, d, w_glu, b_glu, steps):
    n = u_tm.shape[0]
    gp = S5_STATE
    eye_g = jnp.eye(S5_G, dtype=F32)
    b_bd = lambda b: jnp.einsum("gpc,gh->gchp", b, eye_g).reshape(GROUP_W, gp)
    c_bd = lambda c: jnp.einsum("gcp,gh->gphc", c, eye_g).reshape(gp, GROUP_W)
    bmat = jnp.concatenate([b_bd(b_re), b_bd(b_im)], axis=1).astype(BF16)
    cmat = jnp.concatenate([c_bd(c_re), -c_bd(c_im)], axis=0).astype(BF16)
    dt = jnp.repeat(log_dt, S5_P).reshape(1, gp)
    rows = steps * bsz
    kern = functools.partial(_s5_kernel, steps=steps)
    blk = pl.BlockSpec((rows, GROUP_W), lambda i: (i, 0))
    return pl.pallas_call(
        kern,
        grid=(n // rows,),
        in_specs=[blk,
                  _full_spec((1, gp)), _full_spec((1, gp)), _full_spec((1, gp)),
                  _full_spec((GROUP_W, 2 * gp)), _full_spec((2 * gp, GROUP_W)),
                  _full_spec((1, GROUP_W)), _full_spec((GROUP_W, GROUP_W)), _full_spec((1, GROUP_W))],
        out_specs=blk,
        out_shape=jax.ShapeDtypeStruct((n, GROUP_W), F32),
        scratch_shapes=[pltpu.VMEM((rows, 2 * gp), F32), pltpu.VMEM((rows, 2 * gp), F32),
                        pltpu.VMEM((bsz, 2 * gp), F32)],
        compiler_params=_params(("arbitrary",)),
        name="s5_mixer",
    )(u_tm, lam_re.reshape(1, gp), lam_im.reshape(1, gp), dt, bmat, cmat,
      d.reshape(1, GROUP_W), w_glu.astype(BF16), b_glu.reshape(1, GROUP_W))


def _hdot(a_c, b):
    return _bdot(a_c, _stack_heads(b))


def _unit_lower_inverse_apply(a, rhs):
    ti, si = _iota2(a.shape, 0), _iota2(a.shape, 1) % HEAD_DIM
    eye = (ti == si).astype(F32)
    a_d = jnp.where((ti // 16) == (si // 16), a, 0.0)
    a_o = a - a_d
    x = eye - a_d
    p = _hdot(a_d, a_d)
    x = x + _hdot(x, p)
    p = _hdot(p, p)
    x = x + _hdot(x, p)
    p = _hdot(p, p)
    d_inv = x + _hdot(x, p)
    nn = _hdot(d_inv, a_o)
    w = eye - nn
    w = w + _hdot(w, _hdot(nn, nn))
    return _hdot(w, _hdot(d_inv, rhs))


def _rwkv_kernel(z_ref, mu_ref, w0_ref, w2_ref, a0_ref, a2_ref, g2_ref, kk_ref, ka_ref, rk_ref,
                 lng_ref, lnb_ref, y_ref, prev_scr, n_scr):
    @pl.when(pl.program_id(1) == 0)
    def _():
        prev_scr[...] = jnp.zeros_like(prev_scr)
        n_scr[...] = jnp.zeros_like(n_scr)

    params = tuple(ref[...] for ref in (mu_ref, w0_ref, w2_ref, a0_ref, a2_ref, g2_ref, kk_ref, ka_ref,
                                        rk_ref, lng_ref, lnb_ref))
    chunk = jax.vmap(lambda z, prev, n0, tri: _rwkv_chunk(z, prev, n0, tri, params))
    tri = _batched_tri(z_ref.shape[0])
    prev, state = prev_scr[...], n_scr[...]
    for c in range(z_ref.shape[1] // CHUNK):
        rows = slice(c * CHUNK, (c + 1) * CHUNK)
        y_ref[:, rows, :], state = chunk(z_ref[:, rows, :], prev, state, tri)
        prev = z_ref[:, (c + 1) * CHUNK - 8:(c + 1) * CHUNK, :]
    prev_scr[...] = prev
    n_scr[...] = state


def _batched_tri(nb):
    shape = (nb, CHUNK, CHUNK)
    return (_iota2(shape, 1) >= _iota2(shape, 2)).astype(F32)


def _rwkv_chunk(z, prev, n0, tri, params):
    mu, w0, w2, a0, a2, g2, k_k, k_a, r_k, ln_g, ln_b = params
    L = CHUNK
    zs = _shift_rows(z, prev, 1)
    zm = z + (zs - z) * mu
    r, k, v = zm[:, 0:256], zm[:, 256:512], zm[:, 512:768]
    slab, xg = zm[:, 768:768 + RW_LORA], zm[:, 896:1024]

    w_raw = -_softplus(-(w0 + _bdot(jnp.tanh(slab), w2))) - 0.5
    lw = -jnp.exp(w_raw)
    a = _sigmoid(a0 + _bdot(slab, a2))
    g = _bdot(_sigmoid(xg), g2)

    hones = _head_ones()
    kk = k * k_k
    kt = k * (1.0 + (a - 1.0) * k_a)
    sums = _bdot(jnp.concatenate([kk * kk, r * kt * r_k], axis=0), hones)
    kkn = kk / jnp.maximum(jnp.sqrt(sums[:L]), 1e-12)
    bonus = sums[L:]
    beta = kkn * a

    cum = _sdot_left(tri, lw)
    cum_last = cum[L - 1:L, :]
    e_neg = jnp.exp(-cum)
    e_tail = jnp.exp(cum_last - cum)
    qr = jnp.concatenate([kkn * jnp.exp(cum - lw), r * jnp.exp(cum)], axis=0)
    kkb_s = jnp.concatenate([_stack_heads(kt * e_neg), _stack_heads(beta * e_neg)], axis=0)
    prod = _bdot_nt(qr, kkb_s)
    ti, si = _iota2((L, GROUP_W), 0), _iota2((L, GROUP_W), 1) % HEAD_DIM
    strict, incl = ti > si, ti >= si
    a_k = jnp.where(strict, prod[:L, :GROUP_W], 0.0)
    a_b = jnp.where(strict, prod[:L, GROUP_W:], 0.0)
    b_k = jnp.where(incl, prod[L:, :GROUP_W], 0.0)
    b_b = jnp.where(incl, prod[L:, GROUP_W:], 0.0)

    reads = _bdot_nt(qr, n0)
    v_s = _stack_heads(v)
    u_w = _unit_lower_inverse_apply(a_b, reads[:L] + _bdot(a_k, v_s))
    y = reads[L:] + _bdot(b_k, v_s) - _hdot(b_b, u_w)

    same_head = (_iota2((GROUP_W, GROUP_W), 0) // HEAD_DIM) == (_iota2((GROUP_W, GROUP_W), 1) // HEAD_DIM)
    upd = _bdot_tn(jnp.concatenate([v, -u_w], axis=0),
                   jnp.concatenate([kt * e_tail, beta * e_tail], axis=0))
    n_new = n0 * jnp.exp(cum_last) + jnp.where(same_head, upd, 0.0)

    inv_hd = 1.0 / HEAD_DIM
    mean = _sdot(y, hones) * inv_hd
    yc = y - mean
    var = _sdot(yc * yc, hones) * inv_hd
    yn = yc * lax.rsqrt(var + RW_GN_EPS) * ln_g + ln_b
    return (yn + bonus * v) * g, n_new


def _rwkv_call(z_rw, mu, w0, w2, a0, a2, g2, k_k, k_a, r_k, ln_g, ln_b, nb=8, chunks=4):
    bsz, seq, width = z_rw.shape
    rows = chunks * CHUNK
    row = lambda t: t.reshape(1, -1)
    zeros = jnp.zeros((RW_LORA // 2, GROUP_W), F32)
    w2p = jnp.concatenate([w2, zeros], axis=0).astype(BF16)
    a2p = jnp.concatenate([zeros, a2], axis=0).astype(BF16)
    vec = _full_spec((1, GROUP_W))
    mat = _full_spec((RW_LORA, GROUP_W))
    return pl.pallas_call(
        _rwkv_kernel,
        grid=(bsz // nb, seq // rows),
        in_specs=[pl.BlockSpec((nb, rows, width), lambda b, i: (b, i, 0)),
                  _full_spec((1, width)), vec, mat, vec, mat, mat, vec, vec, vec, vec, vec],
        out_specs=pl.BlockSpec((nb, rows, GROUP_W), lambda b, i: (b, i, 0)),
        out_shape=jax.ShapeDtypeStruct((bsz, seq, GROUP_W), F32),
        scratch_shapes=[pltpu.VMEM((nb, 8, width), F32), pltpu.VMEM((nb, GROUP_W, GROUP_W), F32)],
        compiler_params=_params(("arbitrary", "arbitrary")),
        name="rwkv7_mixer",
    )(z_rw, row(mu), row(w0), w2p, row(a0), a2p, g2.astype(BF16), row(k_k), row(k_a), row(r_k),
      row(ln_g), row(ln_b))


def _sb_kernel(q_ref, k_ref, v_ref, o_ref, *, blk):
    qi = pl.program_id(1)
    n4 = N_HEADS * blk
    q_s = _stack_heads(q_ref[...] * (HEAD_DIM ** -0.5)).astype(BF16)
    later = (_iota2((blk, blk), 0) > _iota2((blk, blk), 1)).astype(F32)

    def key_blocks(kbs, carry, acc, diagonal):
        starts = [pl.multiple_of(kb * blk, blk) for kb in kbs]
        logits = [_bdot_nt(q_s, k_ref[pl.ds(s, blk), :]) for s in starts]
        sp = [_softplus(l) for l in logits]
        sp_m = list(sp)
        if diagonal:
            mask = _iota2((n4, blk), 1) < _iota2((n4, blk), 0) % blk
            sp_m[0] = jnp.where(mask, sp[0], 0.0)
        within = [_sdot(s, later) for s in sp_m]
        for i in range(len(kbs)):
            wgt = jnp.exp(logits[i] - sp[i] - (carry + within[i]))
            if diagonal and i == 0:
                wgt = jnp.where(mask, wgt, 0.0)
            wgt_w = jnp.concatenate([wgt[h * blk:(h + 1) * blk] for h in range(N_HEADS)], axis=1)
            acc = acc + _bdot(wgt_w, _stack_heads(v_ref[pl.ds(starts[i], blk), :]))
            carry = carry + jnp.sum(sp_m[i], axis=1, keepdims=True)
        return carry, acc

    def cond(state):
        kb, carry, _ = state
        return jnp.logical_and(kb >= 0, jnp.min(carry) < SB_SKIP_LOG)

    def body(state):
        kb, carry, acc = state
        carry, acc = key_blocks([kb], carry, acc, diagonal=False)
        return kb - 1, carry, acc

    zero = (jnp.zeros((n4, 1), F32), jnp.zeros((blk, GROUP_W), F32))
    carry, acc = lax.cond(qi > 0,
                          lambda: key_blocks([qi, qi - 1], *zero, diagonal=True),
                          lambda: key_blocks([qi], *zero, diagonal=True))
    _, _, acc = lax.while_loop(cond, body, (qi - 2, carry, acc))
    o_ref[...] = acc


def _sb_call(z_sb, blk=256):
    bsz, seq, _ = z_sb.shape
    kern = functools.partial(_sb_kernel, blk=blk)
    return pl.pallas_call(
        kern,
        grid=(bsz, seq // blk),
        in_specs=[pl.BlockSpec((None, blk, GROUP_W), lambda b, i: (b, i, 0)),
                  pl.BlockSpec((None, seq, GROUP_W), lambda b, i: (b, 0, 1)),
                  pl.BlockSpec((None, seq, GROUP_W), lambda b, i: (b, 0, 2))],
        out_specs=pl.BlockSpec((None, blk, GROUP_W), lambda b, i: (b, i, 0)),
        out_shape=jax.ShapeDtypeStruct((bsz, seq, GROUP_W), F32),
        compiler_params=_params(("parallel", "arbitrary")),
        name="stickbreak_attn",
    )(z_sb, z_sb, z_sb)


def _mlstm_kernel(z_ref, g_ref, cw_ref, cb_ref, bi_ref, bf_ref, lng_ref, y_ref,
                  prev_scr, c_scr, n_scr, m_scr):
    @pl.when(pl.program_id(1) == 0)
    def _():
        prev_scr[...] = jnp.zeros_like(prev_scr)
        c_scr[...] = jnp.zeros_like(c_scr)
        n_scr[...] = jnp.zeros_like(n_scr)
        m_scr[...] = jnp.full(m_scr.shape, -jnp.inf, F32)

    params = tuple(ref[...] for ref in (cw_ref, cb_ref, bi_ref, bf_ref, lng_ref))
    chunk = jax.vmap(lambda z, gz, prev, state, tri: _mlstm_chunk(z, gz, prev, state, tri, params))
    tri = _batched_tri(z_ref.shape[0])
    prev, state = prev_scr[...], (c_scr[...], n_scr[:, 0:1, :], m_scr[:, 0:1, :])
    for c in range(z_ref.shape[1] // CHUNK):
        rows = slice(c * CHUNK, (c + 1) * CHUNK)
        y_ref[:, rows, :], state = chunk(z_ref[:, rows, :], g_ref[:, rows, :], prev, state, tri)
        prev = z_ref[:, (c + 1) * CHUNK - 8:(c + 1) * CHUNK, 0:2 * GROUP_W]
    prev_scr[...] = prev
    c_scr[...] = state[0]
    n_scr[...] = jnp.broadcast_to(state[1], n_scr.shape)
    m_scr[...] = jnp.broadcast_to(state[2], m_scr.shape)


def _mlstm_chunk(z, gz, prev, state, tri, params):
    cw, cb, b_i, b_f, ln_g = params
    c_prev, n_prev, m_prev = state
    L = CHUNK
    qk_raw = z[:, 0:2 * GROUP_W]
    conv = qk_raw * cw[3:4] + cb
    for j in range(3):
        conv = conv + _shift_rows(qk_raw, prev, 3 - j) * cw[j:j + 1]
    qk = conv * _sigmoid(conv)
    q, k = qk[:, :GROUP_W], qk[:, GROUP_W:] * (HEAD_DIM ** -0.5)
    v, o = z[:, 2 * GROUP_W:3 * GROUP_W], z[:, 3 * GROUP_W:]

    gate_col = _iota2((gz.shape[1], GROUP_W), 0)
    gate_head = _iota2((gz.shape[1], GROUP_W), 1) // HEAD_DIM
    ig = _sdot(gz, (gate_col == gate_head).astype(F32)) + b_i
    log_f = -_softplus(-(gz + b_f))
    log_f = _sdot(log_f, (gate_col == gate_head + N_HEADS).astype(F32))

    b = _xdot_left(tri, log_f)
    b_last = b[L - 1:L, :]

    gl = b_last - b + ig
    m_loc = jnp.max(gl, axis=0, keepdims=True)
    kw = k * jnp.exp(gl - m_loc)
    same_head = (_iota2((GROUP_W, GROUP_W), 0) // HEAD_DIM) == (_iota2((GROUP_W, GROUP_W), 1) // HEAD_DIM)
    c_loc = jnp.where(same_head, _bdot_tn(kw, v), 0.0)
    n_loc = jnp.sum(kw, axis=0, keepdims=True)
    m_new = jnp.maximum(b_last + m_prev, m_loc)
    s_old, s_new = jnp.exp(b_last + m_prev - m_new), jnp.exp(m_loc - m_new)
    new_state = (s_old * c_prev + s_new * c_loc, s_old * n_prev + s_new * n_loc, m_new)

    lane = _iota2((L, GROUP_W), 1)
    lane_head, lane_s = lane // HEAD_DIM, lane % HEAD_DIM
    row_l = _iota2((L, GROUP_W), 0)
    x = ig - b
    x_row = _xdot_left(jnp.maximum(tri, 1.0), jnp.where(row_l == lane_s, x, 0.0))
    log_d = b + x_row
    causal = lane_s <= row_l
    m_intra = jnp.full((L, GROUP_W), -jnp.inf, F32)
    for h in range(N_HEADS):
        sel = lane_head == h
        mh = jnp.max(jnp.where(jnp.logical_and(sel, causal), log_d, -jnp.inf), axis=1, keepdims=True)
        m_intra = jnp.where(sel, mh, m_intra)
    log_inter = b + m_prev
    m_t = jnp.maximum(log_inter, m_intra)
    w_intra = jnp.where(causal, jnp.exp(log_d - m_t), 0.0)
    w_inter = jnp.exp(log_inter - m_t)

    hones = _head_ones()
    s_mat = _bdot_nt(q, _stack_heads(k)) * w_intra
    num = w_inter * _bdot(q, c_prev) + _bdot(s_mat, _stack_heads(v))
    den = w_inter * _bdot(q * n_prev, hones) + _bdot(s_mat, hones)
    hval = num / jnp.maximum(jnp.abs(den), jnp.exp(-m_t))

    inv_hd = 1.0 / HEAD_DIM
    mean = _sdot(hval, hones) * inv_hd
    hc = hval - mean
    var = _sdot(hc * hc, hones) * inv_hd
    return _sigmoid(o) * (hc * lax.rsqrt(var + LN_EPS) * ln_g), new_state


def _mlstm_call(z_ml, z_gate, conv_w, conv_b, b_i, b_f, ln_g, nb=8, chunks=4):
    bsz, seq, width = z_ml.shape
    gw = z_gate.shape[2]
    rows = chunks * CHUNK
    lanes = lambda t: jnp.repeat(t, HEAD_DIM).reshape(1, GROUP_W)
    b_f_cols = jnp.pad(b_f, (N_HEADS, gw - 2 * N_HEADS)).reshape(1, gw)
    vec = _full_spec((1, GROUP_W))
    return pl.pallas_call(
        _mlstm_kernel,
        grid=(bsz // nb, seq // rows),
        in_specs=[pl.BlockSpec((nb, rows, width), lambda b, i: (b, i, 0)),
                  pl.BlockSpec((nb, rows, gw), lambda b, i: (b, i, 0)),
                  _full_spec(conv_w.shape), _full_spec((1, 2 * GROUP_W)), vec, _full_spec((1, gw)), vec],
        out_specs=pl.BlockSpec((nb, rows, GROUP_W), lambda b, i: (b, i, 0)),
        out_shape=jax.ShapeDtypeStruct((bsz, seq, GROUP_W), F32),
        scratch_shapes=[pltpu.VMEM((nb, 8, 2 * GROUP_W), F32), pltpu.VMEM((nb, GROUP_W, GROUP_W), F32),
                        pltpu.VMEM((nb, 8, GROUP_W), F32), pltpu.VMEM((nb, 8, GROUP_W), F32)],
        compiler_params=_params(("arbitrary", "arbitrary")),
        name="mlstm_mixer",
    )(z_ml, z_gate, conv_w, conv_b.reshape(1, -1), lanes(b_i), b_f_cols, ln_g.reshape(1, -1))


def _resident_spec(shape):
    nd = len(shape)
    return pl.BlockSpec(shape, lambda *_: (0,) * nd, pipeline_mode=pl.Buffered(1))


def _mix_ffn_kernel(h_ref, y0_ref, y1_ref, y2_ref, y3_ref, wout_ref, g1_ref, b1_ref,
                    wup_ref, cw_ref, cb_ref, wdn_ref, g_ref, b_ref, o_ref, prev_scr, *, alpha, d_ff):
    @pl.when(pl.program_id(1) == 0)
    def _():
        prev_scr[...] = jnp.zeros_like(prev_scr)

    acc = alpha * h_ref[...]
    for i, y_ref in enumerate((y0_ref, y1_ref, y2_ref, y3_ref)):
        acc = acc + jnp.dot(y_ref[...].astype(BF16), wout_ref[i * GROUP_W:(i + 1) * GROUP_W, :],
                            preferred_element_type=F32)
    h = _layer_norm(acc, g1_ref[...], b1_ref[...])

    rows = h.shape[0]
    up = jnp.dot(h.astype(BF16), wup_ref[...], preferred_element_type=F32)
    u, gate = up[:, :d_ff], up[:, d_ff:]
    prev = prev_scr[...]
    prev_scr[...] = u[rows - 8:rows]
    cw = cw_ref[...]
    conv = u * cw[2:3] + cb_ref[...]
    conv = conv + _shift_rows(u, prev, 1) * cw[1:2] + _shift_rows(u, prev, 2) * cw[0:1]
    act = (_gelu_tanh(conv) * gate).astype(BF16)
    down = jnp.dot(act, wdn_ref[...], preferred_element_type=F32)
    o_ref[...] = _layer_norm(alpha * h + down, g_ref[...], b_ref[...])


def _mix_ffn_call(h, ys, w_out, g1, b1, w_up, conv_w, conv_b, w_down, g2, b2, alpha, rows=512):
    bsz, seq, d = h.shape
    d_ff = w_down.shape[0]
    blk = lambda w: pl.BlockSpec((None, rows, w), lambda bi, i: (bi, i, 0))
    vec = _full_spec((1, d))
    return pl.pallas_call(
        functools.partial(_mix_ffn_kernel, alpha=alpha, d_ff=d_ff),
        grid=(bsz, seq // rows),
        in_specs=[blk(d), pl.BlockSpec((rows, GROUP_W), lambda bi, i: (i, bi))] + [blk(GROUP_W)] * 3
        + [_resident_spec(w_out.shape), vec, vec,
           _resident_spec(w_up.shape), _full_spec(conv_w.shape), _full_spec((1, d_ff)),
           _resident_spec(w_down.shape), vec, vec],
        out_specs=blk(d),
        out_shape=jax.ShapeDtypeStruct((bsz, seq, d), F32),
        scratch_shapes=[pltpu.VMEM((8, d_ff), F32)],
        compiler_params=_params(("arbitrary", "arbitrary")),
        name="mix_ffn_ln",
    )(h, *ys, w_out.astype(BF16), g1.reshape(1, d), b1.reshape(1, d),
      w_up.astype(BF16), conv_w, conv_b.reshape(1, d_ff), w_down.astype(BF16),
      g2.reshape(1, d), b2.reshape(1, d))


DEPTH = 2
DN_ALPHA = (2 * DEPTH) ** 0.25
IN_SIZES = (256, 1024, 768, 1024, 8)
GATE_PAD = 128
S5_STEPS = 128
ROW_TILE = 512


def kernel(x, ln_in_g, ln_in_b, w_in, s5_lambda_re, s5_lambda_im, s5_log_dt, s5_b_re, s5_b_im, s5_c_re, s5_c_im, s5_d, s5_w_glu, s5_b_glu, rw_mu, rw_w0, rw_w2, rw_a0, rw_a2, rw_g2, rw_k_k, rw_k_a, rw_r_k, rw_ln_g, rw_ln_b, ml_conv_w, ml_conv_b, ml_b_i, ml_b_f, ml_ln_g, w_out, ln1_g, ln1_b, ffn_w_up, ffn_conv_w, ffn_conv_b, ffn_w_down, ln2_g, ln2_b):
    bsz, seq, d = x.shape
    n = bsz * seq
    h = x.reshape(n, d)
    for l in range(DEPTH):
        widths = IN_SIZES[:4] + (GATE_PAD,)
        if l == 0:
            h, z_s5, z_rw, z_sb, z_ml, z_g = _inproj_call(h, w_in, l, widths, ROW_TILE, bsz,
                                                          ln=(ln_in_g, ln_in_b))
        else:
            z_s5, z_rw, z_sb, z_ml, z_g = _inproj_call(h, w_in, l, widths, ROW_TILE, bsz)

        y_s5 = _s5_call(z_s5.reshape(n, GROUP_W), bsz, s5_lambda_re[l], s5_lambda_im[l], s5_log_dt[l],
                        s5_b_re[l], s5_b_im[l], s5_c_re[l], s5_c_im[l], s5_d[l], s5_w_glu[l], s5_b_glu[l],
                        S5_STEPS).reshape(seq, bsz * GROUP_W)
        y_rw = _rwkv_call(z_rw.reshape(bsz, seq, -1), rw_mu[l], rw_w0[l], rw_w2[l], rw_a0[l], rw_a2[l],
                          rw_g2[l], rw_k_k[l], rw_k_a[l], rw_r_k[l], rw_ln_g[l], rw_ln_b[l])
        y_sb = _sb_call(z_sb.reshape(bsz, seq, -1))
        y_ml = _mlstm_call(z_ml.reshape(bsz, seq, -1), z_g.reshape(bsz, seq, -1), ml_conv_w[l],
                           ml_conv_b[l], ml_b_i[l], ml_b_f[l], ml_ln_g[l])
        h = _mix_ffn_call(h.reshape(bsz, seq, d), [y_s5, y_rw, y_sb, y_ml], w_out[l], ln1_g[l], ln1_b[l],
                          ffn_w_up[l], ffn_conv_w[l], ffn_conv_b[l], ffn_w_down[l], ln2_g[l], ln2_b[l],
                          DN_ALPHA).reshape(n, d)
    return h.reshape(bsz, seq, d)
```
